```python
import math
import jax
import jax.numpy as jnp
from jax import lax
import numpy as np

D_MODEL = 1024
BATCH = 4
SEQ = 4096
DEPTH = 2

N_MIXERS = 2
POOL_WINDOWS = (2, 4, 8, 16)
N_POOL_GROUPS = len(POOL_WINDOWS)
POOL_GROUP_DIM = D_MODEL // N_POOL_GROUPS
HEAD_DIM = 64
N_HEADS = D_MODEL // HEAD_DIM
DIL_CONFIGS = ((128, 1), (512, 4), (2048, 16))
N_DIL_GROUPS = len(DIL_CONFIGS)
ATTN_WIDTH = N_HEADS * HEAD_DIM
QKV_WIDTH = N_DIL_GROUPS * 3 * ATTN_WIDTH
D_FF = 2816
MACARON_WEIGHT = 0.5
ALPHA = (2.0 * DEPTH) ** 0.25
BETA = (8.0 * DEPTH) ** -0.25
LN_EPS = 1e-5
MASK_VALUE = -1e30

kernel_name = "hybrid_pool_dilated_attn_macaron_deepnorm"


def _alibi_slopes():
    n = N_DIL_GROUPS * N_HEADS
    s = 2.0 ** (-8.0 * np.arange(1, n + 1) / n)
    return s.reshape(N_DIL_GROUPS, N_HEADS).astype(np.float32)


def _layer_norm(x, g, b):
    x32 = x.astype(jnp.float32)
    mu = jnp.mean(x32, axis=-1, keepdims=True)
    var = jnp.mean(jnp.square(x32 - mu), axis=-1, keepdims=True)
    y = (x32 - mu) * lax.rsqrt(var + LN_EPS)
    return (y * g.astype(jnp.float32) + b.astype(jnp.float32)).astype(x.dtype)


def _swiglu(x, w_gate, w_up, w_down):
    return (jax.nn.silu(x @ w_gate) * (x @ w_up)) @ w_down


def _pool_mixer(x, w_in, w_group, scale, w_out):
    B, S, _ = x.shape
    u = (x @ w_in).reshape(B, S, N_POOL_GROUPS, POOL_GROUP_DIM).astype(jnp.float32)
    csum = jnp.concatenate([jnp.zeros_like(u[:, :1]), jnp.cumsum(u, axis=1)], axis=1)
    half = jnp.asarray([w // 2 for w in POOL_WINDOWS], dtype=jnp.int32)
    t = jnp.arange(S, dtype=jnp.int32)[:, None]
    lo = jnp.clip(t - half[None, :], 0, S)
    hi = jnp.clip(t + half[None, :], 0, S)
    gidx = jnp.arange(N_POOL_GROUPS)[None, :]
    win_sum = csum[:, hi, gidx] - csum[:, lo, gidx]
    mean = win_sum / (hi - lo).astype(jnp.float32)[None, :, :, None]
    mixed = (mean - u).astype(x.dtype)
    y = jnp.einsum('bsgc,gce->bsge', mixed, w_group).reshape(B, S, D_MODEL) * scale
    return y @ w_out


def _dilated_group(q, k, v, window, dilation, slopes):
    B, S, H, E = q.shape
    d = dilation
    L = S // d
    R = window // (2 * d)
    W = R
    nb = -(-L // W)
    Lp = nb * W

    def to_sub(a):
        return a.reshape(B, L, d, H, E).transpose(0, 2, 3, 1, 4)

    qs = jnp.pad(to_sub(q), ((0, 0),) * 3 + ((0, Lp - L), (0, 0))).reshape(B, d, H, nb, W, E)

    def windows(a):
        ap = jnp.pad(to_sub(a), ((0, 0),) * 3 + ((W, W + Lp - L), (0, 0)))
        ap = ap.reshape(B, d, H, nb + 2, W, E)
        return jnp.concatenate([ap[:, :, :, :-2], ap[:, :, :, 1:-1], ap[:, :, :, 2:]], axis=4)

    kw = windows(k)
    vw = windows(v)
    a_idx = jnp.arange(W)
    c_idx = jnp.arange(3 * W)
    n_idx = jnp.arange(nb)
    rel = c_idx[None, :] - W - a_idx[:, None]
    j = (n_idx[:, None, None] - 1) * W + c_idx[None, None, :]
    valid = (jnp.abs(rel)[None] <= R) & (j >= 0) & (j < L)
    dist = (d * jnp.abs(rel)).astype(jnp.float32)
    bias = -slopes[:, None, None] * dist[None]
    scores = jnp.einsum('bdhnqe,bdhnke->bdhnqk', qs.astype(jnp.float32),
                        kw.astype(jnp.float32)) * (E ** -0.5) + bias[:, None]
    scores = jnp.where(valid, scores, MASK_VALUE)
    lse = jax.nn.logsumexp(scores, axis=-1)
    p = jnp.exp(scores - lse[..., None])
    o = jnp.einsum('bdhnqk,bdhnke->bdhnqe', p, vw.astype(jnp.float32))
    o = o.reshape(B, d, H, Lp, E)[:, :, :, :L].transpose(0, 3, 1, 2, 4).reshape(B, S, H, E)
    lse = lse.reshape(B, d, H, Lp)[..., :L].transpose(0, 3, 1, 2).reshape(B, S, H)
    return o, lse


def _dilated_attention_mixer(x, w_qkv, w_out):
    B, S, _ = x.shape
    qkv = (x @ w_qkv).reshape(B, S, N_DIL_GROUPS, 3, N_HEADS, HEAD_DIM)
    slopes = jnp.asarray(_alibi_slopes())
    outs, lses = [], []
    for g, (window, dil) in enumerate(DIL_CONFIGS):
        o, l = _dilated_group(qkv[:, :, g, 0], qkv[:, :, g, 1], qkv[:, :, g, 2],
                              window, dil, slopes[g])
        outs.append(o)
        lses.append(l)
    wts = jax.nn.softmax(jnp.stack(lses), axis=0)
    o = jnp.sum(wts[..., None] * jnp.stack(outs), axis=0)
    return o.reshape(B, S, ATTN_WIDTH).astype(x.dtype) @ w_out


def setup_inputs(seed: int = 0) -> dict:
    key = jax.random.key(seed)
    ks = jax.random.split(key, 16)
    n_pool = (DEPTH + 1) // 2
    n_attn = DEPTH // 2
    f32 = jnp.float32

    def nrm(k, shape, scale):
        return jax.random.normal(k, shape, f32) * scale

    return {
        "x": jax.random.normal(ks[0], (BATCH, SEQ, D_MODEL), f32),
        "ffn1_w_gate": nrm(ks[1], (DEPTH, D_MODEL, D_FF), D_MODEL ** -0.5),
        "ffn1_w_up": nrm(ks[2], (DEPTH, D_MODEL, D_FF), D_MODEL ** -0.5),
        "ffn1_w_down": nrm(ks[3], (DEPTH, D_FF, D_MODEL), BETA * D_FF ** -0.5),
        "ffn2_w_gate": nrm(ks[4], (DEPTH, D_MODEL, D_FF), D_MODEL ** -0.5),
        "ffn2_w_up": nrm(ks[5], (DEPTH, D_MODEL, D_FF), D_MODEL ** -0.5),
        "ffn2_w_down": nrm(ks[6], (DEPTH, D_FF, D_MODEL), BETA * D_FF ** -0.5),
        "ln_gain": 1.0 + nrm(ks[7], (DEPTH, 3, D_MODEL), 0.02),
        "ln_bias": nrm(ks[8], (DEPTH, 3, D_MODEL), 0.02),
        "pool_w_in": nrm(ks[9], (n_pool, D_MODEL, D_MODEL), D_MODEL ** -0.5),
        "pool_w_group": nrm(ks[10], (n_pool, N_POOL_GROUPS, POOL_GROUP_DIM, POOL_GROUP_DIM),
                            POOL_GROUP_DIM ** -0.5),
        "pool_scale": 1.0 + nrm(ks[11], (n_pool, D_MODEL), 0.1),
        "pool_w_out": nrm(ks[12], (n_pool, D_MODEL, D_MODEL), BETA * D_MODEL ** -0.5),
        "attn_w_qkv": nrm(ks[13], (n_attn, D_MODEL, QKV_WIDTH), D_MODEL ** -0.5),
        "attn_w_out": nrm(ks[14], (n_attn, ATTN_WIDTH, D_MODEL), BETA * ATTN_WIDTH ** -0.5),
    }


def reference(x, ffn1_w_gate, ffn1_w_up, ffn1_w_down, ffn2_w_gate, ffn2_w_up, ffn2_w_down,
              ln_gain, ln_bias, pool_w_in, pool_w_group, pool_scale, pool_w_out,
              attn_w_qkv, attn_w_out):
    for i in range(DEPTH):
        h = _swiglu(x, ffn1_w_gate[i], ffn1_w_up[i], ffn1_w_down[i])
        x = _layer_norm(ALPHA * x + MACARON_WEIGHT * h, ln_gain[i, 0], ln_bias[i, 0])
        if i % N_MIXERS == 0:
            p = i // N_MIXERS
            m = _pool_mixer(x, pool_w_in[p], pool_w_group[p], pool_scale[p], pool_w_out[p])
        else:
            a = i // N_MIXERS
            m = _dilated_attention_mixer(x, attn_w_qkv[a], attn_w_out[a])
        x = _layer_norm(ALPHA * x + m, ln_gain[i, 1], ln_bias[i, 1])
        h = _swiglu(x, ffn2_w_gate[i], ffn2_w_up[i], ffn2_w_down[i])
        x = _layer_norm(ALPHA * x + MACARON_WEIGHT * h, ln_gain[i, 2], ln_bias[i, 2])
    return x
```

```python
import functools

import numpy as np
import jax
import jax.numpy as jnp
from jax import lax
from jax.experimental import pallas as pl
from jax.experimental.pallas import tpu as pltpu

D_MODEL = 1024
DEPTH = 2
POOL_WINDOWS = (2, 4, 8, 16)
N_POOL_GROUPS = len(POOL_WINDOWS)
POOL_GROUP_DIM = D_MODEL // N_POOL_GROUPS
POOL_HALO = max(POOL_WINDOWS) // 2
HEAD_DIM = 64
N_HEADS = D_MODEL // HEAD_DIM
DIL_CONFIGS = ((128, 1), (512, 4), (2048, 16))
N_DIL_GROUPS = len(DIL_CONFIGS)
ATTN_WIDTH = N_HEADS * HEAD_DIM
QKV_WIDTH = N_DIL_GROUPS * 3 * ATTN_WIDTH
D_FF = 2816
MACARON_WEIGHT = 0.5
ALPHA = (2.0 * DEPTH) ** 0.25
LN_EPS = 1e-5
MASK_VALUE = -1e30

V7X_LANES = 128
V7X_MXU_DIM = 256
V7X_VMEM_LIMIT_BYTES = 56 * 1024 * 1024

FFN_ROW_TILE = 512
FFN_CHUNKS = ((0, 6 * V7X_MXU_DIM), (6 * V7X_MXU_DIM, D_FF))
POOL_ROW_TILE = 512
QKV_ROW_TILE = 512
ATTN_Q_TILE = 256
ATTN_Q_BLOCK = 128
ATTN_K_BLOCK = 256
OUT_ROW_TILE = 512
LSE_WIDTH = V7X_LANES


def _alibi_slopes():
    n = N_DIL_GROUPS * N_HEADS
    s = 2.0 ** (-8.0 * np.arange(1, n + 1) / n)
    return s.reshape(N_DIL_GROUPS, N_HEADS)


def _layer_norm(y, gain, bias):
    mu = jnp.mean(y, axis=-1, keepdims=True)
    yc = y - mu
    var = jnp.mean(yc * yc, axis=-1, keepdims=True)
    return yc * lax.rsqrt(var + LN_EPS) * gain + bias


def _dot(a, b):
    return jnp.dot(a, b, preferred_element_type=jnp.float32)


def _const_spec(shape):
    return pl.BlockSpec(shape, lambda *_: (0,) * len(shape), pipeline_mode=pl.Buffered(1))


def _params(n_grid_dims):
    return pltpu.CompilerParams(
        dimension_semantics=("arbitrary",) * n_grid_dims,
        vmem_limit_bytes=V7X_VMEM_LIMIT_BYTES)


def _ffn_ln_body(x_ref, wg_ref, wu_ref, wd_ref, gain_ref, bias_ref, o_ref):
    x = x_ref[...]
    xb = x.astype(jnp.bfloat16)
    h = None
    for lo, hi in FFN_CHUNKS:
        g = _dot(xb, wg_ref[:, lo:hi])
        u = _dot(xb, wu_ref[:, lo:hi])
        hg = 0.5 * g
        a = ((hg + hg * jnp.tanh(hg)) * u).astype(jnp.bfloat16)
        part = _dot(a, wd_ref[lo:hi, :])
        h = part if h is None else h + part
    y = ALPHA * x + MACARON_WEIGHT * h
    o_ref[...] = _layer_norm(y, gain_ref[...], bias_ref[...])


def _ffn_ln(x, w_gate, w_up, w_down, gain, bias):
    m = x.shape[0]
    tm = FFN_ROW_TILE
    row_spec = pl.BlockSpec((tm, D_MODEL), lambda i: (i, 0))
    return pl.pallas_call(
        _ffn_ln_body,
        grid=(m // tm,),
        in_specs=[row_spec,
                  _const_spec((D_MODEL, D_FF)), _const_spec((D_MODEL, D_FF)),
                  _const_spec((D_FF, D_MODEL)),
                  _const_spec((1, D_MODEL)), _const_spec((1, D_MODEL))],
        out_specs=row_spec,
        out_shape=jax.ShapeDtypeStruct((m, D_MODEL), jnp.float32),
        compiler_params=_params(1),
        name="ffn_ln",
    )(x, w_gate, w_up, w_down, gain, bias)


def _pool_ln_body(seq_len, xm_ref, xp_ref, xn_ref, win_ref, wgrp_ref, scale_ref, wout_ref,
                  gain_ref, bias_ref, o_ref, xe_ref, u_ref):
    ts = xm_ref.shape[0]
    halo = POOL_HALO
    s = pl.program_id(1)
    n_s = pl.num_programs(1)
    xe_ref[0:halo, :] = xp_ref[...]
    xe_ref[halo:halo + ts, :] = xm_ref[...]
    xe_ref[halo + ts:, :] = xn_ref[...]
    u_ext = _dot(xe_ref[...].astype(jnp.bfloat16), win_ref[...])
    row = lax.broadcasted_iota(jnp.int32, (ts + 2 * halo, 1), 0)
    outside = ((row < halo) & (s == 0)) | ((row >= halo + ts) & (s == n_s - 1))
    u_ref[...] = jnp.where(outside, 0.0, u_ext)

    t_abs = s * ts + lax.broadcasted_iota(jnp.int32, (ts, 1), 0)
    ys = []
    for g, w in enumerate(POOL_WINDOWS):
        half = w // 2
        cols = slice(g * POOL_GROUP_DIM, (g + 1) * POOL_GROUP_DIM)
        win_sum = u_ref[halo - half:halo - half + ts, cols]
        for j in range(-half + 1, half):
            win_sum = win_sum + u_ref[halo + j:halo + j + ts, cols]
        count = jnp.minimum(t_abs + half, seq_len) - jnp.maximum(t_abs - half, 0)
        mean = win_sum / count.astype(jnp.float32)
        mixed = (mean - u_ref[halo:halo + ts, cols]).astype(jnp.bfloat16)
        ys.append(_dot(mixed, wgrp_ref[g]))
    y = (jnp.concatenate(ys, axis=1) * scale_ref[...]).astype(jnp.bfloat16)
    m = _dot(y, wout_ref[...])
    o_ref[...] = _layer_norm(ALPHA * xm_ref[...] + m, gain_ref[...], bias_ref[...])


def _pool_ln(x, w_in, w_group, scale, w_out, gain, bias):
    b, s_len, _ = x.shape
    ts = POOL_ROW_TILE
    halo = POOL_HALO
    blocks_per_tile = ts // halo
    n_halo_blocks = s_len // halo
    main_spec = pl.BlockSpec((None, ts, D_MODEL), lambda bi, si: (bi, si, 0))
    prev_spec = pl.BlockSpec(
        (None, halo, D_MODEL),
        lambda bi, si: (bi, jnp.maximum(si * blocks_per_tile - 1, 0), 0))
    next_spec = pl.BlockSpec(
        (None, halo, D_MODEL),
        lambda bi, si: (bi, jnp.minimum((si + 1) * blocks_per_tile, n_halo_blocks - 1), 0))
    return pl.pallas_call(
        functools.partial(_pool_ln_body, s_len),
        grid=(b, s_len // ts),
        in_specs=[main_spec, prev_spec, next_spec,
                  _const_spec((D_MODEL, D_MODEL)),
                  _const_spec((N_POOL_GROUPS, POOL_GROUP_DIM, POOL_GROUP_DIM)),
                  _const_spec((1, D_MODEL)),
                  _const_spec((D_MODEL, D_MODEL)),
                  _const_spec((1, D_MODEL)), _const_spec((1, D_MODEL))],
        out_specs=main_spec,
        out_shape=jax.ShapeDtypeStruct(x.shape, jnp.float32),
        scratch_shapes=[pltpu.VMEM((ts + 2 * halo, D_MODEL), jnp.float32),
                        pltpu.VMEM((ts + 2 * halo, D_MODEL), jnp.float32)],
        compiler_params=_params(2),
        name="pool_ln",
    )(x, x, x, w_in, w_group, scale, w_out, gain, bias)


def _qkv_body(x_ref, w_ref, o_ref):
    res = _dot(x_ref[...].astype(jnp.bfloat16), w_ref[...])
    col = lax.broadcasted_iota(jnp.int32, (1, res.shape[1]), 1)
    res = res * jnp.where(col < ATTN_WIDTH, HEAD_DIM ** -0.5, 1.0)
    o_ref[...] = res.astype(jnp.bfloat16)


def _qkv_proj(x, w_qkv):
    m = x.shape[0]
    tm = QKV_ROW_TILE
    gw = 3 * ATTN_WIDTH
    return pl.pallas_call(
        _qkv_body,
        grid=(N_DIL_GROUPS, m // tm),
        in_specs=[pl.BlockSpec((tm, D_MODEL), lambda g, i: (i, 0)),
                  pl.BlockSpec((D_MODEL, gw), lambda g, i: (0, g))],
        out_specs=pl.BlockSpec((tm, gw), lambda g, i: (i, g)),
        out_shape=jax.ShapeDtypeStruct((m, QKV_WIDTH), jnp.bfloat16),
        compiler_params=_params(2),
        name="qkv_proj",
    )(x, w_qkv)


def _attn_body(slopes, dilation, sub_len, q_ref, k_ref, kp_ref, kn_ref, v_ref, vp_ref, vn_ref,
               o_ref, lse_ref, ke_ref, ve_ref):
    tq = q_ref.shape[0]
    qb, kb = ATTN_Q_BLOCK, ATTN_K_BLOCK
    r = (kb - qb) // 2
    i = pl.program_id(2)
    ke_ref[0:r, :] = kp_ref[...]
    ke_ref[r:r + tq, :] = k_ref[...]
    ke_ref[r + tq:, :] = kn_ref[...]
    ve_ref[0:r, :] = vp_ref[...]
    ve_ref[r:r + tq, :] = v_ref[...]
    ve_ref[r + tq:, :] = vn_ref[...]

    a_idx = lax.broadcasted_iota(jnp.int32, (qb, kb), 0)
    c_idx = lax.broadcasted_iota(jnp.int32, (qb, kb), 1)
    rel = c_idx - r - a_idx
    lane = lax.broadcasted_iota(jnp.int32, (qb, V7X_LANES), 1)
    low_half = lane < HEAD_DIM
    for blk in range(tq // qb):
        key_pos = i * tq + blk * qb + c_idx - r
        valid = (jnp.abs(rel) <= r) & (key_pos >= 0) & (key_pos < sub_len)
        neg_dist = jnp.where(valid, -(dilation * jnp.abs(rel)).astype(jnp.float32), MASK_VALUE)
        rows = slice(blk * qb, (blk + 1) * qb)
        krows = slice(blk * qb, blk * qb + kb)
        lse_tile = jnp.zeros((qb, LSE_WIDTH), jnp.float32)
        for pair in range(N_HEADS // 2):
            cols = slice(pair * V7X_LANES, (pair + 1) * V7X_LANES)
            q2 = q_ref[rows, cols]
            zero = jnp.zeros_like(q2)
            q_stack = jnp.concatenate(
                [jnp.where(low_half, q2, zero), jnp.where(low_half, zero, q2)], axis=0)
            s = lax.dot_general(q_stack, ke_ref[krows, cols], (((1,), (1,)), ((), ())),
                                preferred_element_type=jnp.float32)
            s = jnp.concatenate(
                [s[:qb] + slopes[2 * pair] * neg_dist,
                 s[qb:] + slopes[2 * pair + 1] * neg_dist], axis=0)
            m = jnp.max(s, axis=1, keepdims=True)
            p = jnp.exp(s - m)
            l = jnp.sum(p, axis=1, keepdims=True)
            o2 = _dot(p.astype(jnp.bfloat16), ve_ref[krows, cols])
            inv_l = 1.0 / l
            o_ref[rows, cols] = jnp.where(
                low_half, o2[:qb] * inv_l[:qb], o2[qb:] * inv_l[qb:]).astype(o_ref.dtype)
            lse = m + jnp.log(l)
            lse_tile = jnp.where(lane == 2 * pair, lse[:qb], lse_tile)
            lse_tile = jnp.where(lane == 2 * pair + 1, lse[qb:], lse_tile)
        lse_ref[rows, :] = lse_tile


def _attn_group(qkv, group, batch, seq_len):
    window, d = DIL_CONFIGS[group]
    sub_len = seq_len // d
    r = window // (2 * d)
    assert r == (ATTN_K_BLOCK - ATTN_Q_BLOCK) // 2
    tq = min(ATTN_Q_TILE, sub_len)
    n_tiles = sub_len // tq
    halo_per_tile = tq // r
    n_halo_blocks = sub_len // r
    slopes = tuple(float(v) for v in _alibi_slopes()[group])
    n_col_blocks = QKV_WIDTH // ATTN_WIDTH

    def col(j):
        return lambda bi, ri, ti: ri * n_col_blocks + group * 3 + j

    def main_spec(j):
        c = col(j)
        return pl.BlockSpec((None, tq, ATTN_WIDTH), lambda bi, ri, ti: (bi, ti, c(bi, ri, ti)))

    def prev_spec(j):
        c = col(j)
        return pl.BlockSpec(
            (None, r, ATTN_WIDTH),
            lambda bi, ri, ti: (bi, jnp.maximum(ti * halo_per_tile - 1, 0), c(bi, ri, ti)))

    def next_spec(j):
        c = col(j)
        return pl.BlockSpec(
            (None, r, ATTN_WIDTH),
            lambda bi, ri, ti: (bi, jnp.minimum((ti + 1) * halo_per_tile, n_halo_blocks - 1),
                                c(bi, ri, ti)))

    qkv_view = qkv.reshape(batch, sub_len, d * QKV_WIDTH)
    o, lse = pl.pallas_call(
        functools.partial(_attn_body, slopes, d, sub_len),
        grid=(batch, d, n_tiles),
        in_specs=[main_spec(0),
                  main_spec(1), prev_spec(1), next_spec(1),
                  main_spec(2), prev_spec(2), next_spec(2)],
        out_specs=[pl.BlockSpec((None, tq, ATTN_WIDTH), lambda bi, ri, ti: (bi, ti, ri)),
                   pl.BlockSpec((None, tq, LSE_WIDTH), lambda bi, ri, ti: (bi, ti, ri))],
        out_shape=[jax.ShapeDtypeStruct((batch, sub_len, d * ATTN_WIDTH), jnp.bfloat16),
                   jax.ShapeDtypeStruct((batch, sub_len, d * LSE_WIDTH), jnp.float32)],
        scratch_shapes=[pltpu.VMEM((tq + 2 * r, ATTN_WIDTH), jnp.bfloat16),
                        pltpu.VMEM((tq + 2 * r, ATTN_WIDTH), jnp.bfloat16)],
        compiler_params=_params(3),
        name=f"attn_group{group}",
    )(qkv_view, qkv_view, qkv_view, qkv_view, qkv_view, qkv_view, qkv_view)
    return (o.reshape(batch * seq_len, ATTN_WIDTH), lse.reshape(batch * seq_len, LSE_WIDTH))


def _attn_out_ln_body(x_ref, o0_ref, o1_ref, o2_ref, l0_ref, l1_ref, l2_ref, w_ref,
                      gain_ref, bias_ref, out_ref):
    lses = [l0_ref[...], l1_ref[...], l2_ref[...]]
    top = jnp.maximum(jnp.maximum(lses[0], lses[1]), lses[2])
    es = [jnp.exp(l - top) for l in lses]
    inv = 1.0 / (es[0] + es[1] + es[2])
    wts = [e * inv for e in es]
    tm = x_ref.shape[0]
    low_half = lax.broadcasted_iota(jnp.int32, (tm, V7X_LANES), 1) < HEAD_DIM
    parts = []
    for pair in range(N_HEADS // 2):
        cols = slice(pair * V7X_LANES, (pair + 1) * V7X_LANES)
        acc = None
        for wt, o_ref in zip(wts, (o0_ref, o1_ref, o2_ref)):
            w_pair = jnp.where(low_half, wt[:, 2 * pair:2 * pair + 1],
                               wt[:, 2 * pair + 1:2 * pair + 2])
            term = w_pair * o_ref[:, cols].astype(jnp.float32)
            acc = term if acc is None else acc + term
        parts.append(acc.astype(jnp.bfloat16))
    o = jnp.concatenate(parts, axis=1)
    m = _dot(o, w_ref[...])
    out_ref[...] = _layer_norm(ALPHA * x_ref[...] + m, gain_ref[...], bias_ref[...])


def _attn_out_ln(x, os_, lses, w_out, gain, bias):
    m = x.shape[0]
    tm = OUT_ROW_TILE
    row_spec = pl.BlockSpec((tm, D_MODEL), lambda i: (i, 0))
    lse_spec = pl.BlockSpec((tm, LSE_WIDTH), lambda i: (i, 0))
    return pl.pallas_call(
        _attn_out_ln_body,
        grid=(m // tm,),
        in_specs=[row_spec, row_spec, row_spec, row_spec, lse_spec, lse_spec, lse_spec,
                  _const_spec((ATTN_WIDTH, D_MODEL)),
                  _const_spec((1, D_MODEL)), _const_spec((1, D_MODEL))],
        out_specs=row_spec,
        out_shape=jax.ShapeDtypeStruct((m, D_MODEL), jnp.float32),
        compiler_params=_params(1),
        name="attn_out_ln",
    )(x, *os_, *lses, w_out, gain, bias)


def kernel(x, ffn1_w_gate, ffn1_w_up, ffn1_w_down, ffn2_w_gate, ffn2_w_up, ffn2_w_down,
           ln_gain, ln_bias, pool_w_in, pool_w_group, pool_scale, pool_w_out,
           attn_w_qkv, attn_w_out):
    batch, seq_len, _ = x.shape
    bf16 = jnp.bfloat16
    n_mixers = 2
    h = x.reshape(batch * seq_len, D_MODEL)

    def ln_params(i, j):
        return ln_gain[i, j].reshape(1, D_MODEL), ln_bias[i, j].reshape(1, D_MODEL)

    for i in range(DEPTH):
        h = _ffn_ln(h, ffn1_w_gate[i].astype(bf16), ffn1_w_up[i].astype(bf16),
                    ffn1_w_down[i].astype(bf16), *ln_params(i, 0))
        if i % n_mixers == 0:
            p = i // n_mixers
            h = _pool_ln(h.reshape(batch, seq_len, D_MODEL), pool_w_in[p].astype(bf16),
                         pool_w_group[p].astype(bf16), pool_scale[p].reshape(1, D_MODEL),
                         pool_w_out[p].astype(bf16), *ln_params(i, 1))
            h = h.reshape(batch * seq_len, D_MODEL)
        else:
            a = i // n_mixers
            qkv = _qkv_proj(h, attn_w_qkv[a].astype(bf16))
            outs = [_attn_group(qkv, g, batch, seq_len) for g in range(N_DIL_GROUPS)]
            h = _attn_out_ln(h, [o for o, _ in outs], [l for _, l in outs],
                             attn_w_out[a].astype(bf16), *ln_params(i, 1))
        h = _ffn_ln(h, ffn2_w_gate[i].astype(bf16), ffn2_w_up[i].astype(bf16),
                    ffn2_w_down[i].astype(bf16), *ln_params(i, 2))
    return h.reshape(batch, seq_len, D_MODEL)
```

```python
import functools

import numpy as np
import jax
import jax.numpy as jnp
from jax import lax
from jax.experimental import pallas as pl
from jax.experimental.pallas import tpu as pltpu

D_MODEL = 1024
DEPTH = 2
POOL_WINDOWS = (2, 4, 8, 16)
N_POOL_GROUPS = len(POOL_WINDOWS)
POOL_GROUP_DIM = D_MODEL // N_POOL_GROUPS
POOL_HALO = max(POOL_WINDOWS) // 2
HEAD_DIM = 64
N_HEADS = D_MODEL // HEAD_DIM
DIL_CONFIGS = ((128, 1), (512, 4), (2048, 16))
N_DIL_GROUPS = len(DIL_CONFIGS)
ATTN_WIDTH = N_HEADS * HEAD_DIM
QKV_WIDTH = N_DIL_GROUPS * 3 * ATTN_WIDTH
D_FF = 2816
MACARON_WEIGHT = 0.5
ALPHA = (2.0 * DEPTH) ** 0.25
LN_EPS = 1e-5
MASK_VALUE = -1e30

V7X_LANES = 128
V7X_MXU_DIM = 256
V7X_VMEM_LIMIT_BYTES = 56 * 1024 * 1024

FFN_ROW_TILE = 512
FFN_CHUNKS = ((0, 6 * V7X_MXU_DIM), (6 * V7X_MXU_DIM, D_FF))
POOL_ROW_TILE = 512
QKV_ROW_TILE = 512
ATTN_Q_TILE = 256
ATTN_Q_BLOCK = 128
ATTN_K_BLOCK = 256
OUT_ROW_TILE = 512
LSE_WIDTH = V7X_LANES


def _alibi_slopes():
    n = N_DIL_GROUPS * N_HEADS
    s = 2.0 ** (-8.0 * np.arange(1, n + 1) / n)
    return s.reshape(N_DIL_GROUPS, N_HEADS)


def _layer_norm(y, gain, bias):
    mu = jnp.mean(y, axis=-1, keepdims=True)
    yc = y - mu
    var = jnp.mean(yc * yc, axis=-1, keepdims=True)
    return yc * lax.rsqrt(var + LN_EPS) * gain + bias


def _dot(a, b):
    return jnp.dot(a, b, preferred_element_type=jnp.float32)


def _const_spec(shape):
    return pl.BlockSpec(shape, lambda *_: (0,) * len(shape), pipeline_mode=pl.Buffered(1))


def _params(n_grid_dims):
    return pltpu.CompilerParams(
        dimension_semantics=("arbitrary",) * n_grid_dims,
        vmem_limit_bytes=V7X_VMEM_LIMIT_BYTES)


def _ffn_ln_body(x_ref, wg_ref, wu_ref, wd_ref, gain_ref, bias_ref, o_ref):
    x = x_ref[...]
    xb = x.astype(jnp.bfloat16)
    h = None
    for lo, hi in FFN_CHUNKS:
        g = _dot(xb, wg_ref[:, lo:hi])
        u = _dot(xb, wu_ref[:, lo:hi])
        hg = 0.5 * g
        a = ((hg + hg * jnp.tanh(hg)) * u).astype(jnp.bfloat16)
        part = _dot(a, wd_ref[lo:hi, :])
        h = part if h is None else h + part
    y = ALPHA * x + MACARON_WEIGHT * h
    o_ref[...] = _layer_norm(y, gain_ref[...], bias_ref[...])


def _ffn_ln(x, w_gate, w_up, w_down, gain, bias):
    m = x.shape[0]
    tm = FFN_ROW_TILE
    row_spec = pl.BlockSpec((tm, D_MODEL), lambda i: (i, 0))
    return pl.pallas_call(
        _ffn_ln_body,
        grid=(m // tm,),
        in_specs=[row_spec,
                  _const_spec((D_MODEL, D_FF)), _const_spec((D_MODEL, D_FF)),
                  _const_spec((D_FF, D_MODEL)),
                  _const_spec((1, D_MODEL)), _const_spec((1, D_MODEL))],
        out_specs=row_spec,
        out_shape=jax.ShapeDtypeStruct((m, D_MODEL), jnp.float32),
        compiler_params=_params(1),
        name="ffn_ln",
    )(x, w_gate, w_up, w_down, gain, bias)


def _pool_ln_body(seq_len, xm_ref, xp_ref, xn_ref, win_ref, wgrp_ref, scale_ref, wout_ref,
                  gain_ref, bias_ref, o_ref, xe_ref, u_ref):
    ts = xm_ref.shape[0]
    halo = POOL_HALO
    s = pl.program_id(1)
    n_s = pl.num_programs(1)
    xe_ref[0:halo, :] = xp_ref[...]
    xe_ref[halo:halo + ts, :] = xm_ref[...]
    xe_ref[halo + ts:, :] = xn_ref[...]
    u_ext = _dot(xe_ref[...].astype(jnp.bfloat16), win_ref[...])
    row = lax.broadcasted_iota(jnp.int32, (ts + 2 * halo, 1), 0)
    outside = ((row < halo) & (s == 0)) | ((row >= halo + ts) & (s == n_s - 1))
    u_ref[...] = jnp.where(outside, 0.0, u_ext)

    t_abs = s * ts + lax.broadcasted_iota(jnp.int32, (ts, 1), 0)
    ys = []
    for g, w in enumerate(POOL_WINDOWS):
        half = w // 2
        cols = slice(g * POOL_GROUP_DIM, (g + 1) * POOL_GROUP_DIM)
        win_sum = u_ref[halo - half:halo - half + ts, cols]
        for j in range(-half + 1, half):
            win_sum = win_sum + u_ref[halo + j:halo + j + ts, cols]
        count = jnp.minimum(t_abs + half, seq_len) - jnp.maximum(t_abs - half, 0)
        mean = win_sum / count.astype(jnp.float32)
        mixed = (mean - u_ref[halo:halo + ts, cols]).astype(jnp.bfloat16)
        ys.append(_dot(mixed, wgrp_ref[g]))
    y = (jnp.concatenate(ys, axis=1) * scale_ref[...]).astype(jnp.bfloat16)
    m = _dot(y, wout_ref[...])
    o_ref[...] = _layer_norm(ALPHA * xm_ref[...] + m, gain_ref[...], bias_ref[...])


def _pool_ln(x, w_in, w_group, scale, w_out, gain, bias):
    b, s_len, _ = x.shape
    ts = POOL_ROW_TILE
    halo = POOL_HALO
    blocks_per_tile = ts // halo
    n_halo_blocks = s_len // halo
    main_spec = pl.BlockSpec((None, ts, D_MODEL), lambda bi, si: (bi, si, 0))
    prev_spec = pl.BlockSpec(
        (None, halo, D_MODEL),
        lambda bi, si: (bi, jnp.maximum(si * blocks_per_tile - 1, 0), 0))
    next_spec = pl.BlockSpec(
        (None, halo, D_MODEL),
        lambda bi, si: (bi, jnp.minimum((si + 1) * blocks_per_tile, n_halo_blocks - 1), 0))
    return pl.pallas_call(
        functools.partial(_pool_ln_body, s_len),
        grid=(b, s_len // ts),
        in_specs=[main_spec, prev_spec, next_spec,
                  _const_spec((D_MODEL, D_MODEL)),
                  _const_spec((N_POOL_GROUPS, POOL_GROUP_DIM, POOL_GROUP_DIM)),
                  _const_spec((1, D_MODEL)),
                  _const_spec((D_MODEL, D_MODEL)),
                  _const_spec((1, D_MODEL)), _const_spec((1, D_MODEL))],
        out_specs=main_spec,
        out_shape=jax.ShapeDtypeStruct(x.shape, jnp.float32),
        scratch_shapes=[pltpu.VMEM((ts + 2 * halo, D_MODEL), jnp.float32),
                        pltpu.VMEM((ts + 2 * halo, D_MODEL), jnp.float32)],
        compiler_params=_params(2),
        name="pool_ln",
    )(x, x, x, w_in, w_group, scale, w_out, gain, bias)


def _qkv_body(x_ref, w_ref, o0_ref, o1_ref, o2_ref, x_slabs):
    tm = x_ref.shape[0]
    n_slabs = D_MODEL // V7X_LANES
    for c in range(n_slabs):
        x_slabs[c] = x_ref[:, c * V7X_LANES:(c + 1) * V7X_LANES]
    for g, ((_, d), o_ref) in enumerate(zip(DIL_CONFIGS, (o0_ref, o1_ref, o2_ref))):
        n = tm // d
        if d == 1:
            xp = x_ref[...]
        else:
            xp = jnp.concatenate(
                [jnp.concatenate([x_slabs[c, pl.ds(r, n, stride=d), :] for r in range(d)], axis=0)
                 for c in range(n_slabs)], axis=1)
        xp = xp.astype(jnp.bfloat16)
        for j in range(3):
            lo = (g * 3 + j) * ATTN_WIDTH
            res = _dot(xp, w_ref[:, lo:lo + ATTN_WIDTH])
            if j == 0:
                res = res * HEAD_DIM ** -0.5
            res = res.astype(jnp.bfloat16)
            for r in range(d):
                o_ref[r, :, j * ATTN_WIDTH:(j + 1) * ATTN_WIDTH] = res[r * n:(r + 1) * n]


def _qkv_proj(x, w_qkv, batch, seq_len):
    tm = QKV_ROW_TILE
    tiles_per_seq = seq_len // tm
    gw = 3 * ATTN_WIDTH
    return pl.pallas_call(
        _qkv_body,
        grid=(batch * tiles_per_seq,),
        in_specs=[pl.BlockSpec((tm, D_MODEL), lambda i: (i, 0)),
                  _const_spec((D_MODEL, QKV_WIDTH))],
        out_specs=[pl.BlockSpec((None, d, tm // d, gw),
                                lambda i: (i // tiles_per_seq, 0, i % tiles_per_seq, 0))
                   for _, d in DIL_CONFIGS],
        out_shape=[jax.ShapeDtypeStruct((batch, d, seq_len // d, gw), jnp.bfloat16)
                   for _, d in DIL_CONFIGS],
        scratch_shapes=[pltpu.VMEM((D_MODEL // V7X_LANES, tm, V7X_LANES), jnp.float32)],
        compiler_params=_params(1),
        name="qkv_proj",
    )(x, w_qkv)


def _attn_body(slopes, dilation, sub_len, q_ref, k_ref, kp_ref, kn_ref, v_ref, vp_ref, vn_ref,
               o_ref, lse_ref, ke_ref, ve_ref):
    tq = q_ref.shape[0]
    qb, kb = ATTN_Q_BLOCK, ATTN_K_BLOCK
    r = (kb - qb) // 2
    i = pl.program_id(2)
    ke_ref[0:r, :] = kp_ref[...]
    ke_ref[r:r + tq, :] = k_ref[...]
    ke_ref[r + tq:, :] = kn_ref[...]
    ve_ref[0:r, :] = vp_ref[...]
    ve_ref[r:r + tq, :] = v_ref[...]
    ve_ref[r + tq:, :] = vn_ref[...]

    a_idx = lax.broadcasted_iota(jnp.int32, (qb, kb), 0)
    c_idx = lax.broadcasted_iota(jnp.int32, (qb, kb), 1)
    rel = c_idx - r - a_idx
    lane = lax.broadcasted_iota(jnp.int32, (qb, V7X_LANES), 1)
    low_half = lane < HEAD_DIM
    for blk in range(tq // qb):
        key_pos = i * tq + blk * qb + c_idx - r
        valid = (jnp.abs(rel) <= r) & (key_pos >= 0) & (key_pos < sub_len)
        neg_dist = jnp.where(valid, -(dilation * jnp.abs(rel)).astype(jnp.float32), MASK_VALUE)
        rows = slice(blk * qb, (blk + 1) * qb)
        krows = slice(blk * qb, blk * qb + kb)
        lse_tile = jnp.zeros((qb, LSE_WIDTH), jnp.float32)
        for pair in range(N_HEADS // 2):
            cols = slice(pair * V7X_LANES, (pair + 1) * V7X_LANES)
            q2 = q_ref[rows, cols]
            zero = jnp.zeros_like(q2)
            q_stack = jnp.concatenate(
                [jnp.where(low_half, q2, zero), jnp.where(low_half, zero, q2)], axis=0)
            s = lax.dot_general(q_stack, ke_ref[krows, cols], (((1,), (1,)), ((), ())),
                                preferred_element_type=jnp.float32)
            s = jnp.concatenate(
                [s[:qb] + slopes[2 * pair] * neg_dist,
                 s[qb:] + slopes[2 * pair + 1] * neg_dist], axis=0)
            m = jnp.max(s, axis=1, keepdims=True)
            p = jnp.exp(s - m)
            l = jnp.sum(p, axis=1, keepdims=True)
            o2 = _dot(p.astype(jnp.bfloat16), ve_ref[krows, cols])
            inv_l = 1.0 / l
            o_ref[rows, cols] = jnp.where(
                low_half, o2[:qb] * inv_l[:qb], o2[qb:] * inv_l[qb:]).astype(o_ref.dtype)
            lse = m + jnp.log(l)
            lse_tile = jnp.where(lane == 2 * pair, lse[:qb], lse_tile)
            lse_tile = jnp.where(lane == 2 * pair + 1, lse[qb:], lse_tile)
        lse_ref[rows, :] = lse_tile


def _attn_group(qkv_g, group):
    window, d = DIL_CONFIGS[group]
    batch, _, sub_len, _ = qkv_g.shape
    r = window // (2 * d)
    assert r == (ATTN_K_BLOCK - ATTN_Q_BLOCK) // 2
    tq = min(ATTN_Q_TILE, sub_len)
    n_tiles = sub_len // tq
    halo_per_tile = tq // r
    n_halo_blocks = sub_len // r
    slopes = tuple(float(v) for v in _alibi_slopes()[group])

    def main_spec(j, width=ATTN_WIDTH):
        return pl.BlockSpec((None, None, tq, width), lambda bi, ri, ti: (bi, ri, ti, j))

    def prev_spec(j):
        return pl.BlockSpec(
            (None, None, r, ATTN_WIDTH),
            lambda bi, ri, ti: (bi, ri, jnp.maximum(ti * halo_per_tile - 1, 0), j))

    def next_spec(j):
        return pl.BlockSpec(
            (None, None, r, ATTN_WIDTH),
            lambda bi, ri, ti: (bi, ri, jnp.minimum((ti + 1) * halo_per_tile, n_halo_blocks - 1), j))

    return pl.pallas_call(
        functools.partial(_attn_body, slopes, d, sub_len),
        grid=(batch, d, n_tiles),
        in_specs=[main_spec(0),
                  main_spec(1), prev_spec(1), next_spec(1),
                  main_spec(2), prev_spec(2), next_spec(2)],
        out_specs=[main_spec(0), main_spec(0, LSE_WIDTH)],
        out_shape=[jax.ShapeDtypeStruct((batch, d, sub_len, ATTN_WIDTH), jnp.bfloat16),
                   jax.ShapeDtypeStruct((batch, d, sub_len, LSE_WIDTH), jnp.float32)],
        scratch_shapes=[pltpu.VMEM((tq + 2 * r, ATTN_WIDTH), jnp.bfloat16),
                        pltpu.VMEM((tq + 2 * r, ATTN_WIDTH), jnp.bfloat16)],
        compiler_params=_params(3),
        name=f"attn_group{group}",
    )(*([qkv_g] * 7))


def _head_expand_matrix():
    k = np.arange(2 * LSE_WIDTH)[:, None] % LSE_WIDTH
    c = np.arange(ATTN_WIDTH)[None, :] // HEAD_DIM
    return jnp.asarray(k == c, dtype=jnp.bfloat16)


def _attn_out_ln_body(x_ref, o0_ref, o1_ref, o2_ref, l0_ref, l1_ref, l2_ref, e_ref, w_ref,
                      gain_ref, bias_ref, out_ref, o_scr, l_scr):
    o_refs = (o0_ref, o1_ref, o2_ref)
    l_refs = (l0_ref, l1_ref, l2_ref)

    def token_order(ref, scr, g):
        d, n, width = ref.shape
        if d == 1:
            return ref[0].astype(jnp.float32)
        n_slabs = width // V7X_LANES
        for r in range(d):
            piece = ref[r].astype(jnp.float32)
            for c in range(n_slabs):
                scr[g, c, pl.ds(r, n, stride=d), :] = piece[:, c * V7X_LANES:(c + 1) * V7X_LANES]
        return jnp.concatenate([scr[g, c] for c in range(n_slabs)], axis=1)

    lses = [token_order(l_refs[g], l_scr, g) for g in range(N_DIL_GROUPS)]
    top = jnp.maximum(jnp.maximum(lses[0], lses[1]), lses[2])
    es = [jnp.exp(l - top) for l in lses]
    inv = 1.0 / (es[0] + es[1] + es[2])
    acc = None
    for g in range(N_DIL_GROUPS):
        wt = es[g] * inv
        hi = wt.astype(jnp.bfloat16)
        lo = (wt - hi.astype(jnp.float32)).astype(jnp.bfloat16)
        w_full = _dot(jnp.concatenate([hi, lo], axis=1), e_ref[...])
        term = w_full * token_order(o_refs[g], o_scr, g)
        acc = term if acc is None else acc + term
    m = _dot(acc.astype(jnp.bfloat16), w_ref[...])
    out_ref[...] = _layer_norm(ALPHA * x_ref[...] + m, gain_ref[...], bias_ref[...])


def _attn_out_ln(x, os_, lses, w_out, gain, bias, seq_len):
    m = x.shape[0]
    tm = OUT_ROW_TILE
    tiles_per_seq = seq_len // tm
    row_spec = pl.BlockSpec((tm, D_MODEL), lambda i: (i, 0))

    def group_spec(d, width):
        return pl.BlockSpec((None, d, tm // d, width),
                            lambda i: (i // tiles_per_seq, 0, i % tiles_per_seq, 0))

    return pl.pallas_call(
        _attn_out_ln_body,
        grid=(m // tm,),
        in_specs=[row_spec]
                 + [group_spec(d, ATTN_WIDTH) for _, d in DIL_CONFIGS]
                 + [group_spec(d, LSE_WIDTH) for _, d in DIL_CONFIGS]
                 + [_const_spec((2 * LSE_WIDTH, ATTN_WIDTH)),
                    _const_spec((ATTN_WIDTH, D_MODEL)),
                    _const_spec((1, D_MODEL)), _const_spec((1, D_MODEL))],
        out_specs=row_spec,
        out_shape=jax.ShapeDtypeStruct((m, D_MODEL), jnp.float32),
        scratch_shapes=[
            pltpu.VMEM((N_DIL_GROUPS, ATTN_WIDTH // V7X_LANES, tm, V7X_LANES), jnp.float32),
            pltpu.VMEM((N_DIL_GROUPS, LSE_WIDTH // V7X_LANES, tm, V7X_LANES), jnp.float32)],
        compiler_params=_params(1),
        name="attn_out_ln",
    )(x, *os_, *lses, _head_expand_matrix(), w_out, gain, bias)


def kernel(x, ffn1_w_gate, ffn1_w_up, ffn1_w_down, ffn2_w_gate, ffn2_w_up, ffn2_w_down,
           ln_gain, ln_bias, pool_w_in, pool_w_group, pool_scale, pool_w_out,
           attn_w_qkv, attn_w_out):
    batch, seq_len, _ = x.shape
    bf16 = jnp.bfloat16
    n_mixers = 2
    h = x.reshape(batch * seq_len, D_MODEL)

    def ln_params(i, j):
        return ln_gain[i, j].reshape(1, D_MODEL), ln_bias[i, j].reshape(1, D_MODEL)

    for i in range(DEPTH):
        h = _ffn_ln(h, ffn1_w_gate[i].astype(bf16), ffn1_w_up[i].astype(bf16),
                    ffn1_w_down[i].astype(bf16), *ln_params(i, 0))
        if i % n_mixers == 0:
            p = i // n_mixers
            h = _pool_ln(h.reshape(batch, seq_len, D_MODEL), pool_w_in[p].astype(bf16),
                         pool_w_group[p].astype(bf16), pool_scale[p].reshape(1, D_MODEL),
                         pool_w_out[p].astype(bf16), *ln_params(i, 1))
            h = h.reshape(batch * seq_len, D_MODEL)
        else:
            a = i // n_mixers
            qkv = _qkv_proj(h, attn_w_qkv[a].astype(bf16), batch, seq_len)
            outs = [_attn_group(qkv[g], g) for g in range(N_DIL_GROUPS)]
            h = _attn_out_ln(h, [o for o, _ in outs], [l for _, l in outs],
                             attn_w_out[a].astype(bf16), *ln_params(i, 1), seq_len)
        h = _ffn_ln(h, ffn2_w_gate[i].astype(bf16), ffn2_w_up[i].astype(bf16),
                    ffn2_w_down[i].astype(bf16), *ln_params(i, 2))
    return h.reshape(batch, seq_len, D_MODEL)
```

```python
import functools

import numpy as np
import jax
import jax.numpy as jnp
from jax import lax
from jax.experimental import pallas as pl
from jax.experimental.pallas import tpu as pltpu

D_MODEL = 1024
DEPTH = 2
POOL_WINDOWS = (2, 4, 8, 16)
N_POOL_GROUPS = len(POOL_WINDOWS)
POOL_GROUP_DIM = D_MODEL // N_POOL_GROUPS
POOL_HALO = max(POOL_WINDOWS) // 2
HEAD_DIM = 64
N_HEADS = D_MODEL // HEAD_DIM
DIL_CONFIGS = ((128, 1), (512, 4), (2048, 16))
N_DIL_GROUPS = len(DIL_CONFIGS)
ATTN_WIDTH = N_HEADS * HEAD_DIM
QKV_WIDTH = N_DIL_GROUPS * 3 * ATTN_WIDTH
D_FF = 2816
MACARON_WEIGHT = 0.5
ALPHA = (2.0 * DEPTH) ** 0.25
LN_EPS = 1e-5
MASK_VALUE = -1e30
LOG2_E = float(np.log2(np.e))

V7X_LANES = 128
V7X_MXU_DIM = 256
V7X_VMEM_LIMIT_BYTES = 56 * 1024 * 1024

FFN_ROW_TILE = 512
FFN_CHUNKS = ((0, 6 * V7X_MXU_DIM), (6 * V7X_MXU_DIM, D_FF))
POOL_ROW_TILE = 512
QKV_ROW_TILE = 512
ATTN_Q_TILE = 512
ATTN_Q_BLOCK = 128
ATTN_K_BLOCK = 256
OUT_ROW_TILE = 512
STAT_WIDTH = V7X_LANES


def _alibi_slopes():
    n = N_DIL_GROUPS * N_HEADS
    s = 2.0 ** (-8.0 * np.arange(1, n + 1) / n)
    return s.reshape(N_DIL_GROUPS, N_HEADS)


def _layer_norm(y, gain, bias):
    mu = jnp.mean(y, axis=-1, keepdims=True)
    yc = y - mu
    var = jnp.mean(yc * yc, axis=-1, keepdims=True)
    return yc * lax.rsqrt(var + LN_EPS) * gain + bias


def _dot(a, b):
    return jnp.dot(a, b, preferred_element_type=jnp.float32)


def _const_spec(shape):
    return pl.BlockSpec(shape, lambda *_: (0,) * len(shape), pipeline_mode=pl.Buffered(1))


def _params(n_grid_dims):
    return pltpu.CompilerParams(
        dimension_semantics=("arbitrary",) * n_grid_dims,
        vmem_limit_bytes=V7X_VMEM_LIMIT_BYTES)


def _ffn_ln_body(x_ref, wg_ref, wu_ref, wd_ref, gain_ref, bias_ref, o_ref):
    x = x_ref[...]
    xb = x.astype(jnp.bfloat16)
    h = None
    for lo, hi in FFN_CHUNKS:
        g = _dot(xb, wg_ref[:, lo:hi])
        u = _dot(xb, wu_ref[:, lo:hi])
        hg = 0.5 * g
        a = ((hg + hg * jnp.tanh(hg)) * u).astype(jnp.bfloat16)
        part = _dot(a, wd_ref[lo:hi, :])
        h = part if h is None else h + part
    y = ALPHA * x + MACARON_WEIGHT * h
    o_ref[...] = _layer_norm(y, gain_ref[...], bias_ref[...])


def _ffn_ln(x, w_gate, w_up, w_down, gain, bias):
    m = x.shape[0]
    tm = FFN_ROW_TILE
    row_spec = pl.BlockSpec((tm, D_MODEL), lambda i: (i, 0))
    return pl.pallas_call(
        _ffn_ln_body,
        grid=(m // tm,),
        in_specs=[row_spec,
                  _const_spec((D_MODEL, D_FF)), _const_spec((D_MODEL, D_FF)),
                  _const_spec((D_FF, D_MODEL)),
                  _const_spec((1, D_MODEL)), _const_spec((1, D_MODEL))],
        out_specs=row_spec,
        out_shape=jax.ShapeDtypeStruct((m, D_MODEL), jnp.float32),
        compiler_params=_params(1),
        name="ffn_ln",
    )(x, w_gate, w_up, w_down, gain, bias)


def _pool_ln_body(seq_len, xm_ref, xp_ref, xn_ref, win_ref, wgrp_ref, scale_ref, wout_ref,
                  gain_ref, bias_ref, o_ref, xe_ref, u_ref):
    ts = xm_ref.shape[0]
    halo = POOL_HALO
    s = pl.program_id(1)
    n_s = pl.num_programs(1)
    xe_ref[0:halo, :] = xp_ref[...]
    xe_ref[halo:halo + ts, :] = xm_ref[...]
    xe_ref[halo + ts:, :] = xn_ref[...]
    u_ext = _dot(xe_ref[...].astype(jnp.bfloat16), win_ref[...])
    row = lax.broadcasted_iota(jnp.int32, (ts + 2 * halo, 1), 0)
    outside = ((row < halo) & (s == 0)) | ((row >= halo + ts) & (s == n_s - 1))
    u_ref[...] = jnp.where(outside, 0.0, u_ext)

    t_abs = s * ts + lax.broadcasted_iota(jnp.int32, (ts, 1), 0)
    ys = []
    for g, w in enumerate(POOL_WINDOWS):
        half = w // 2
        cols = slice(g * POOL_GROUP_DIM, (g + 1) * POOL_GROUP_DIM)
        win_sum = u_ref[halo - half:halo - half + ts, cols]
        for j in range(-half + 1, half):
            win_sum = win_sum + u_ref[halo + j:halo + j + ts, cols]
        count = jnp.minimum(t_abs + half, seq_len) - jnp.maximum(t_abs - half, 0)
        mean = win_sum / count.astype(jnp.float32)
        mixed = (mean - u_ref[halo:halo + ts, cols]).astype(jnp.bfloat16)
        ys.append(_dot(mixed, wgrp_ref[g]))
    y = (jnp.concatenate(ys, axis=1) * scale_ref[...]).astype(jnp.bfloat16)
    m = _dot(y, wout_ref[...])
    o_ref[...] = _layer_norm(ALPHA * xm_ref[...] + m, gain_ref[...], bias_ref[...])


def _pool_ln(x, w_in, w_group, scale, w_out, gain, bias):
    b, s_len, _ = x.shape
    ts = POOL_ROW_TILE
    halo = POOL_HALO
    blocks_per_tile = ts // halo
    n_halo_blocks = s_len // halo
    main_spec = pl.BlockSpec((None, ts, D_MODEL), lambda bi, si: (bi, si, 0))
    prev_spec = pl.BlockSpec(
        (None, halo, D_MODEL),
        lambda bi, si: (bi, jnp.maximum(si * blocks_per_tile - 1, 0), 0))
    next_spec = pl.BlockSpec(
        (None, halo, D_MODEL),
        lambda bi, si: (bi, jnp.minimum((si + 1) * blocks_per_tile, n_halo_blocks - 1), 0))
    return pl.pallas_call(
        functools.partial(_pool_ln_body, s_len),
        grid=(b, s_len // ts),
        in_specs=[main_spec, prev_spec, next_spec,
                  _const_spec((D_MODEL, D_MODEL)),
                  _const_spec((N_POOL_GROUPS, POOL_GROUP_DIM, POOL_GROUP_DIM)),
                  _const_spec((1, D_MODEL)),
                  _const_spec((D_MODEL, D_MODEL)),
                  _const_spec((1, D_MODEL)), _const_spec((1, D_MODEL))],
        out_specs=main_spec,
        out_shape=jax.ShapeDtypeStruct(x.shape, jnp.float32),
        scratch_shapes=[pltpu.VMEM((ts + 2 * halo, D_MODEL), jnp.float32),
                        pltpu.VMEM((ts + 2 * halo, D_MODEL), jnp.float32)],
        compiler_params=_params(2),
        name="pool_ln",
    )(x, x, x, w_in, w_group, scale, w_out, gain, bias)


def _qkv_body(x_ref, w_ref, o0_ref, o1_ref, o2_ref, x_slabs):
    tm = x_ref.shape[0]
    n_slabs = D_MODEL // V7X_LANES
    for c in range(n_slabs):
        x_slabs[c] = x_ref[:, c * V7X_LANES:(c + 1) * V7X_LANES]
    for g, ((_, d), o_ref) in enumerate(zip(DIL_CONFIGS, (o0_ref, o1_ref, o2_ref))):
        n = tm // d
        if d == 1:
            xp = x_ref[...]
        else:
            xp = jnp.concatenate(
                [jnp.concatenate([x_slabs[c, pl.ds(r, n, stride=d), :] for r in range(d)], axis=0)
                 for c in range(n_slabs)], axis=1)
        xp = xp.astype(jnp.bfloat16)
        for j in range(3):
            lo = (g * 3 + j) * ATTN_WIDTH
            res = _dot(xp, w_ref[:, lo:lo + ATTN_WIDTH])
            if j == 0:
                res = res * (HEAD_DIM ** -0.5 * LOG2_E)
            res = res.astype(jnp.bfloat16)
            for r in range(d):
                o_ref[r, :, j * ATTN_WIDTH:(j + 1) * ATTN_WIDTH] = res[r * n:(r + 1) * n]


def _qkv_proj(x, w_qkv, batch, seq_len):
    tm = QKV_ROW_TILE
    tiles_per_seq = seq_len // tm
    gw = 3 * ATTN_WIDTH
    return pl.pallas_call(
        _qkv_body,
        grid=(batch * tiles_per_seq,),
        in_specs=[pl.BlockSpec((tm, D_MODEL), lambda i: (i, 0)),
                  _const_spec((D_MODEL, QKV_WIDTH))],
        out_specs=[pl.BlockSpec((None, d, tm // d, gw),
                                lambda i: (i // tiles_per_seq, 0, i % tiles_per_seq, 0))
                   for _, d in DIL_CONFIGS],
        out_shape=[jax.ShapeDtypeStruct((batch, d, seq_len // d, gw), jnp.bfloat16)
                   for _, d in DIL_CONFIGS],
        scratch_shapes=[pltpu.VMEM((D_MODEL // V7X_LANES, tm, V7X_LANES), jnp.float32)],
        compiler_params=_params(1),
        name="qkv_proj",
    )(x, w_qkv)


def _attn_bias(group):
    window, d = DIL_CONFIGS[group]
    qb, kb = ATTN_Q_BLOCK, ATTN_K_BLOCK
    r = (kb - qb) // 2
    a = np.arange(qb)[:, None]
    c = np.arange(kb)[None, :]
    rel = c - r - a
    band = np.abs(rel) <= r
    valid = np.stack([band, band & (c >= r), band & (c < qb + r)])
    neg_dist = np.where(valid, -(d * np.abs(rel)).astype(np.float64), MASK_VALUE).astype(np.float32)
    slopes = (_alibi_slopes()[group] * LOG2_E).astype(np.float32)
    return jnp.asarray(slopes)[None, :, None, None] * jnp.asarray(neg_dist)[:, None]


def _attn_body(n_blocks_total, q_ref, k_ref, kp_ref, kn_ref, v_ref, vp_ref, vn_ref, bias_ref,
               o_ref, stat_ref, ke_ref, ve_ref, s_ref, p_ref):
    tq = q_ref.shape[0]
    qb, kb = ATTN_Q_BLOCK, ATTN_K_BLOCK
    r = (kb - qb) // 2
    hq = qb // 2
    n_blk = tq // qb
    ti = pl.program_id(2)
    ke_ref[0:r, :] = kp_ref[...]
    ke_ref[r:r + tq, :] = k_ref[...]
    ke_ref[r + tq:, :] = kn_ref[...]
    ve_ref[0:r, :] = vp_ref[...]
    ve_ref[r:r + tq, :] = v_ref[...]
    ve_ref[r + tq:, :] = vn_ref[...]

    low_half = lax.broadcasted_iota(jnp.int32, (qb, V7X_LANES), 1) < HEAD_DIM
    lane = lax.broadcasted_iota(jnp.int32, (hq, STAT_WIDTH), 1)

    for blk in range(n_blk):
        row0 = blk * qb
        g_blk = ti * n_blk + blk
        variant = jnp.where(g_blk == 0, 1, 0) + jnp.where(g_blk == n_blocks_total - 1, 2, 0)
        stats = [jnp.zeros((hq, STAT_WIDTH), jnp.float32) for _ in range(2)]
        for pair in range(N_HEADS // 2):
            cols = slice(pair * V7X_LANES, (pair + 1) * V7X_LANES)
            buf = pair % 2
            q2 = q_ref[pl.ds(row0, qb), cols]
            zero = jnp.zeros_like(q2)
            q_stack = jnp.concatenate(
                [jnp.where(low_half, q2, zero), jnp.where(low_half, zero, q2)], axis=0)
            s_ref[buf] = lax.dot_general(q_stack, ke_ref[pl.ds(row0, kb), cols],
                                         (((1,), (1,)), ((), ())),
                                         preferred_element_type=jnp.float32)
            for c in range(4):
                head, half = 2 * pair + c // 2, c % 2
                rows = slice(c * hq, (c + 1) * hq)
                sc = s_ref[buf, rows, :] + bias_ref[variant, head, half * hq:(half + 1) * hq, :]
                m = jnp.max(sc, axis=1, keepdims=True)
                p = jnp.exp2(sc - m)
                l = jnp.sum(p, axis=1, keepdims=True)
                p_ref[buf, rows, :] = p.astype(jnp.bfloat16)
                stats[half] = jnp.where(lane == head, m, stats[half])
                stats[half] = jnp.where(lane == N_HEADS + head, l, stats[half])
            o2 = _dot(p_ref[buf], ve_ref[pl.ds(row0, kb), cols])
            o_ref[pl.ds(row0, qb), cols] = jnp.where(low_half, o2[:qb], o2[qb:]).astype(o_ref.dtype)
        for half in range(2):
            stat_ref[pl.ds(row0 + half * hq, hq), :] = stats[half]


def _attn_group(qkv_g, group):
    window, d = DIL_CONFIGS[group]
    batch, _, sub_len, _ = qkv_g.shape
    qb, kb = ATTN_Q_BLOCK, ATTN_K_BLOCK
    r = window // (2 * d)
    assert r == (kb - qb) // 2 and sub_len >= 2 * qb
    tq = min(ATTN_Q_TILE, sub_len)
    n_tiles = sub_len // tq
    halo_per_tile = tq // r
    n_halo_blocks = sub_len // r

    def main_spec(j, width=ATTN_WIDTH):
        return pl.BlockSpec((None, None, tq, width), lambda bi, ri, ti: (bi, ri, ti, j))

    def prev_spec(j):
        return pl.BlockSpec(
            (None, None, r, ATTN_WIDTH),
            lambda bi, ri, ti: (bi, ri, jnp.maximum(ti * halo_per_tile - 1, 0), j))

    def next_spec(j):
        return pl.BlockSpec(
            (None, None, r, ATTN_WIDTH),
            lambda bi, ri, ti: (bi, ri, jnp.minimum((ti + 1) * halo_per_tile, n_halo_blocks - 1), j))

    return pl.pallas_call(
        functools.partial(_attn_body, sub_len // qb),
        grid=(batch, d, n_tiles),
        in_specs=[main_spec(0),
                  main_spec(1), prev_spec(1), next_spec(1),
                  main_spec(2), prev_spec(2), next_spec(2),
                  _const_spec((3, N_HEADS, qb, kb))],
        out_specs=[main_spec(0), main_spec(0, STAT_WIDTH)],
        out_shape=[jax.ShapeDtypeStruct((batch, d, sub_len, ATTN_WIDTH), jnp.bfloat16),
                   jax.ShapeDtypeStruct((batch, d, sub_len, STAT_WIDTH), jnp.float32)],
        scratch_shapes=[pltpu.VMEM((tq + 2 * r, ATTN_WIDTH), jnp.bfloat16),
                        pltpu.VMEM((tq + 2 * r, ATTN_WIDTH), jnp.bfloat16),
                        pltpu.VMEM((2, 2 * qb, kb), jnp.float32),
                        pltpu.VMEM((2, 2 * qb, kb), jnp.bfloat16)],
        compiler_params=_params(3),
        name=f"attn_group{group}",
    )(*([qkv_g] * 7), _attn_bias(group))


def _head_expand_matrix():
    k = np.arange(2 * STAT_WIDTH)[:, None] % STAT_WIDTH
    c = np.arange(ATTN_WIDTH)[None, :] // HEAD_DIM
    return jnp.asarray(k == c, dtype=jnp.bfloat16)


def _attn_out_ln_body(x_ref, o0_ref, o1_ref, o2_ref, l0_ref, l1_ref, l2_ref, e_ref, w_ref,
                      gain_ref, bias_ref, out_ref, o_scr, l_scr):
    o_refs = (o0_ref, o1_ref, o2_ref)
    l_refs = (l0_ref, l1_ref, l2_ref)

    def token_order(ref, scr, g):
        d, n, width = ref.shape
        if d == 1:
            return ref[0].astype(jnp.float32)
        n_slabs = width // V7X_LANES
        for r in range(d):
            piece = ref[r].astype(jnp.float32)
            for c in range(n_slabs):
                scr[g, c, pl.ds(r, n, stride=d), :] = piece[:, c * V7X_LANES:(c + 1) * V7X_LANES]
        return jnp.concatenate([scr[g, c] for c in range(n_slabs)], axis=1)

    ms = [token_order(l_refs[g], l_scr, g) for g in range(N_DIL_GROUPS)]
    ls = [pltpu.roll(m, STAT_WIDTH - N_HEADS, 1) for m in ms]
    top = jnp.maximum(jnp.maximum(ms[0], ms[1]), ms[2])
    es = [jnp.exp2(m - top) for m in ms]
    inv = 1.0 / (es[0] * ls[0] + es[1] * ls[1] + es[2] * ls[2])
    is_head = lax.broadcasted_iota(jnp.int32, inv.shape, 1) < N_HEADS
    acc = None
    for g in range(N_DIL_GROUPS):
        wt = jnp.where(is_head, es[g] * inv, 0.0)
        hi = wt.astype(jnp.bfloat16)
        lo = (wt - hi.astype(jnp.float32)).astype(jnp.bfloat16)
        w_full = _dot(jnp.concatenate([hi, lo], axis=1), e_ref[...])
        term = w_full * token_order(o_refs[g], o_scr, g)
        acc = term if acc is None else acc + term
    m = _dot(acc.astype(jnp.bfloat16), w_ref[...])
    out_ref[...] = _layer_norm(ALPHA * x_ref[...] + m, gain_ref[...], bias_ref[...])


def _attn_out_ln(x, os_, stats, w_out, gain, bias, seq_len):
    m = x.shape[0]
    tm = OUT_ROW_TILE
    tiles_per_seq = seq_len // tm
    row_spec = pl.BlockSpec((tm, D_MODEL), lambda i: (i, 0))

    def group_spec(d, width):
        return pl.BlockSpec((None, d, tm // d, width),
                            lambda i: (i // tiles_per_seq, 0, i % tiles_per_seq, 0))

    return pl.pallas_call(
        _attn_out_ln_body,
        grid=(m // tm,),
        in_specs=[row_spec]
                 + [group_spec(d, ATTN_WIDTH) for _, d in DIL_CONFIGS]
                 + [group_spec(d, STAT_WIDTH) for _, d in DIL_CONFIGS]
                 + [_const_spec((2 * STAT_WIDTH, ATTN_WIDTH)),
                    _const_spec((ATTN_WIDTH, D_MODEL)),
                    _const_spec((1, D_MODEL)), _const_spec((1, D_MODEL))],
        out_specs=row_spec,
        out_shape=jax.ShapeDtypeStruct((m, D_MODEL), jnp.float32),
        scratch_shapes=[
            pltpu.VMEM((N_DIL_GROUPS, ATTN_WIDTH // V7X_LANES, tm, V7X_LANES), jnp.float32),
            pltpu.VMEM((N_DIL_GROUPS, STAT_WIDTH // V7X_LANES, tm, V7X_LANES), jnp.float32)],
        compiler_params=_params(1),
        name="attn_out_ln",
    )(x, *os_, *stats, _head_expand_matrix(), w_out, gain, bias)


def kernel(x, ffn1_w_gate, ffn1_w_up, ffn1_w_down, ffn2_w_gate, ffn2_w_up, ffn2_w_down,
           ln_gain, ln_bias, pool_w_in, pool_w_group, pool_scale, pool_w_out,
           attn_w_qkv, attn_w_out):
    batch, seq_len, _ = x.shape
    bf16 = jnp.bfloat16
    n_mixers = 2
    h = x.reshape(batch * seq_len, D_MODEL)

    def ln_params(i, j):
        return ln_gain[i, j].reshape(1, D_MODEL), ln_bias[i, j].reshape(1, D_MODEL)

    for i in range(DEPTH):
        h = _ffn_ln(h, ffn1_w_gate[i].astype(bf16), ffn1_w_up[i].astype(bf16),
                    ffn1_w_down[i].astype(bf16), *ln_params(i, 0))
        if i % n_mixers == 0:
            p = i // n_mixers
            h = _pool_ln(h.reshape(batch, seq_len, D_MODEL), pool_w_in[p].astype(bf16),
                         pool_w_group[p].astype(bf16), pool_scale[p].reshape(1, D_MODEL),
                         pool_w_out[p].astype(bf16), *ln_params(i, 1))
            h = h.reshape(batch * seq_len, D_MODEL)
        else:
            a = i // n_mixers
            qkv = _qkv_proj(h, attn_w_qkv[a].astype(bf16), batch, seq_len)
            outs = [_attn_group(qkv[g], g) for g in range(N_DIL_GROUPS)]
            h = _attn_out_ln(h, [o for o, _ in outs], [l for _, l in outs],
                             attn_w_out[a].astype(bf16), *ln_params(i, 1), seq_len)
        h = _ffn_ln(h, ffn2_w_gate[i].astype(bf16), ffn2_w_up[i].astype(bf16),
                    ffn2_w_down[i].astype(bf16), *ln_params(i, 2))
    return h.reshape(batch, seq_len, D_MODEL)
```

```python
import functools

import numpy as np
import jax
import jax.numpy as jnp
from jax import lax
from jax.experimental import pallas as pl
from jax.experimental.pallas import tpu as pltpu

D_MODEL = 1024
DEPTH = 2
POOL_WINDOWS = (2, 4, 8, 16)
N_POOL_GROUPS = len(POOL_WINDOWS)
POOL_GROUP_DIM = D_MODEL // N_POOL_GROUPS
POOL_HALO = max(POOL_WINDOWS) // 2
HEAD_DIM = 64
N_HEADS = D_MODEL // HEAD_DIM
DIL_CONFIGS = ((128, 1), (512, 4), (2048, 16))
N_DIL_GROUPS = len(DIL_CONFIGS)
ATTN_WIDTH = N_HEADS * HEAD_DIM
QKV_WIDTH = N_DIL_GROUPS * 3 * ATTN_WIDTH
D_FF = 2816
MACARON_WEIGHT = 0.5
ALPHA = (2.0 * DEPTH) ** 0.25
LN_EPS = 1e-5
MASK_VALUE = -1e30
LOG2_E = float(np.log2(np.e))

V7X_LANES = 128
V7X_MXU_DIM = 256
V7X_SUBLANES = 8
V7X_VMEM_LIMIT_BYTES = 56 * 1024 * 1024

FFN_ROW_TILE = 1024
FFN_SUB_TILE = 256
FFN_CHUNKS = ((0, 6 * V7X_MXU_DIM), (6 * V7X_MXU_DIM, D_FF))
POOL_ROW_TILE = 1024
POOL_SUB_TILE = 256
QKV_ROW_TILE = 512
ATTN_Q_TILE = 512
ATTN_Q_BLOCK = 128
ATTN_K_BLOCK = 256
OUT_ROW_TILE = 1024
OUT_SUB_TILE = 256
STAT_WIDTH = V7X_LANES


def _alibi_slopes():
    n = N_DIL_GROUPS * N_HEADS
    s = 2.0 ** (-8.0 * np.arange(1, n + 1) / n)
    return s.reshape(N_DIL_GROUPS, N_HEADS)


def _layer_norm(y, gain, bias):
    mu = jnp.mean(y, axis=-1, keepdims=True)
    yc = y - mu
    var = jnp.mean(yc * yc, axis=-1, keepdims=True)
    return yc * lax.rsqrt(var + LN_EPS) * gain + bias


def _dot(a, b):
    return jnp.dot(a, b, preferred_element_type=jnp.float32)


def _const_spec(shape):
    return pl.BlockSpec(shape, lambda *_: (0,) * len(shape), pipeline_mode=pl.Buffered(1))


def _params(n_grid_dims):
    return pltpu.CompilerParams(
        dimension_semantics=("arbitrary",) * n_grid_dims,
        vmem_limit_bytes=V7X_VMEM_LIMIT_BYTES)


def _ffn_ln_body(x_ref, wg_ref, wu_ref, wd_ref, gain_ref, bias_ref, o_ref):
    for r0 in range(0, x_ref.shape[0], FFN_SUB_TILE):
        rows = slice(r0, r0 + FFN_SUB_TILE)
        x = x_ref[rows, :]
        xb = x.astype(jnp.bfloat16)
        h = None
        for lo, hi in FFN_CHUNKS:
            g = _dot(xb, wg_ref[:, lo:hi])
            u = _dot(xb, wu_ref[:, lo:hi])
            hg = 0.5 * g
            a = ((hg + hg * jnp.tanh(hg)) * u).astype(jnp.bfloat16)
            part = _dot(a, wd_ref[lo:hi, :])
            h = part if h is None else h + part
        y = ALPHA * x + MACARON_WEIGHT * h
        o_ref[rows, :] = _layer_norm(y, gain_ref[...], bias_ref[...])


def _ffn_ln(x, w_gate, w_up, w_down, gain, bias, layer):
    m = x.shape[0]
    tm = FFN_ROW_TILE
    row_spec = pl.BlockSpec((tm, D_MODEL), lambda i: (i, 0))

    def layer_spec(rows, cols):
        return pl.BlockSpec((None, rows, cols), lambda i: (layer, 0, 0),
                            pipeline_mode=pl.Buffered(1))

    return pl.pallas_call(
        _ffn_ln_body,
        grid=(m // tm,),
        in_specs=[row_spec,
                  layer_spec(D_MODEL, D_FF), layer_spec(D_MODEL, D_FF),
                  layer_spec(D_FF, D_MODEL),
                  _const_spec((1, D_MODEL)), _const_spec((1, D_MODEL))],
        out_specs=row_spec,
        out_shape=jax.ShapeDtypeStruct((m, D_MODEL), jnp.float32),
        compiler_params=_params(1),
        name="ffn_ln",
    )(x, w_gate, w_up, w_down, gain, bias)


def _pool_ln_body(seq_len, xm_ref, xp_ref, xn_ref, win_ref, wgrp_ref, scale_ref, wout_ref,
                  gain_ref, bias_ref, o_ref, xe_ref, u_ref, lvl_ref):
    ts = xm_ref.shape[0]
    halo = POOL_HALO
    s = pl.program_id(1)
    n_s = pl.num_programs(1)
    xe_ref[0:halo, :] = xp_ref[...]
    xe_ref[halo:halo + ts, :] = xm_ref[...]
    xe_ref[halo + ts:, :] = xn_ref[...]
    tb = POOL_SUB_TILE
    n_ext = tb + 2 * halo
    n_sub = ts // tb
    for k in range(n_sub):
        r0 = k * tb
        u_ext = _dot(xe_ref[r0:r0 + n_ext, :].astype(jnp.bfloat16), win_ref[...])
        row = lax.broadcasted_iota(jnp.int32, (n_ext, 1), 0)
        if k == 0:
            u_ext = jnp.where((row < halo) & (s == 0), 0.0, u_ext)
        if k == n_sub - 1:
            u_ext = jnp.where((row >= halo + tb) & (s == n_s - 1), 0.0, u_ext)
        u_ref[k, 0:n_ext, :] = u_ext
        u_ref[k, n_ext:, :] = jnp.zeros((V7X_SUBLANES, D_MODEL), jnp.float32)

        t_abs = s * ts + r0 + lax.broadcasted_iota(jnp.int32, (tb, 1), 0)
        ys = []
        lvl = 0
        for g, w in enumerate(POOL_WINDOWS):
            half = w // 2
            cols = slice(g * POOL_GROUP_DIM, (g + 1) * POOL_GROUP_DIM)
            src = functools.partial(lambda k_, c_, lo, n: u_ref[k_, lo:lo + n, c_], k, cols)
            width = 1
            while 2 * width < w:
                lvl_ref[k, lvl, 0:n_ext, :] = src(0, n_ext) + src(width, n_ext)
                lvl_ref[k, lvl, n_ext:, :] = jnp.zeros((V7X_SUBLANES, POOL_GROUP_DIM), jnp.float32)
                src = functools.partial(lambda k_, l_, lo, n: lvl_ref[k_, l_, lo:lo + n, :], k, lvl)
                width, lvl = 2 * width, lvl + 1
            start = halo - half
            win_sum = src(start, tb) + src(start + width, tb)
            count = jnp.minimum(t_abs + half, seq_len) - jnp.maximum(t_abs - half, 0)
            mean = win_sum / count.astype(jnp.float32)
            mixed = (mean - u_ref[k, halo:halo + tb, cols]).astype(jnp.bfloat16)
            ys.append(_dot(mixed, wgrp_ref[g]))
        y = (jnp.concatenate(ys, axis=1) * scale_ref[...]).astype(jnp.bfloat16)
        m = _dot(y, wout_ref[...])
        o_ref[r0:r0 + tb, :] = _layer_norm(ALPHA * xm_ref[r0:r0 + tb, :] + m,
                                           gain_ref[...], bias_ref[...])


def _pool_ln(x, w_in, w_group, scale, w_out, gain, bias):
    b, s_len, _ = x.shape
    ts = POOL_ROW_TILE
    halo = POOL_HALO
    blocks_per_tile = ts // halo
    n_halo_blocks = s_len // halo
    n_levels = sum(int(np.log2(w)) - 1 for w in POOL_WINDOWS)
    tb = POOL_SUB_TILE
    buf_rows = tb + 2 * halo + V7X_SUBLANES
    main_spec = pl.BlockSpec((None, ts, D_MODEL), lambda bi, si: (bi, si, 0))
    prev_spec = pl.BlockSpec(
        (None, halo, D_MODEL),
        lambda bi, si: (bi, jnp.maximum(si * blocks_per_tile - 1, 0), 0))
    next_spec = pl.BlockSpec(
        (None, halo, D_MODEL),
        lambda bi, si: (bi, jnp.minimum((si + 1) * blocks_per_tile, n_halo_blocks - 1), 0))
    return pl.pallas_call(
        functools.partial(_pool_ln_body, s_len),
        grid=(b, s_len // ts),
        in_specs=[main_spec, prev_spec, next_spec,
                  _const_spec((D_MODEL, D_MODEL)),
                  _const_spec((N_POOL_GROUPS, POOL_GROUP_DIM, POOL_GROUP_DIM)),
                  _const_spec((1, D_MODEL)),
                  _const_spec((D_MODEL, D_MODEL)),
                  _const_spec((1, D_MODEL)), _const_spec((1, D_MODEL))],
        out_specs=main_spec,
        out_shape=jax.ShapeDtypeStruct(x.shape, jnp.float32),
        scratch_shapes=[pltpu.VMEM((ts + 2 * halo, D_MODEL), jnp.float32),
                        pltpu.VMEM((ts // tb, buf_rows, D_MODEL), jnp.float32),
                        pltpu.VMEM((ts // tb, n_levels, buf_rows, POOL_GROUP_DIM), jnp.float32)],
        compiler_params=_params(2),
        name="pool_ln",
    )(x, x, x, w_in, w_group, scale, w_out, gain, bias)


def _qkv_body(x_ref, w_ref, o0_ref, o1_ref, o2_ref, x_slabs):
    tm = x_ref.shape[0]
    n_slabs = D_MODEL // V7X_LANES
    for c in range(n_slabs):
        x_slabs[c] = x_ref[:, c * V7X_LANES:(c + 1) * V7X_LANES]
    for g, ((_, d), o_ref) in enumerate(zip(DIL_CONFIGS, (o0_ref, o1_ref, o2_ref))):
        n = tm // d
        if d == 1:
            xp = x_ref[...]
        else:
            xp = jnp.concatenate(
                [jnp.concatenate([x_slabs[c, pl.ds(r, n, stride=d), :] for r in range(d)], axis=0)
                 for c in range(n_slabs)], axis=1)
        xp = xp.astype(jnp.bfloat16)
        for j in range(3):
            lo = (g * 3 + j) * ATTN_WIDTH
            res = _dot(xp, w_ref[:, lo:lo + ATTN_WIDTH])
            if j == 0:
                res = res * (HEAD_DIM ** -0.5 * LOG2_E)
            res = res.astype(jnp.bfloat16)
            for r in range(d):
                o_ref[r, :, j * ATTN_WIDTH:(j + 1) * ATTN_WIDTH] = res[r * n:(r + 1) * n]


def _qkv_proj(x, w_qkv, batch, seq_len):
    tm = QKV_ROW_TILE
    tiles_per_seq = seq_len // tm
    gw = 3 * ATTN_WIDTH
    return pl.pallas_call(
        _qkv_body,
        grid=(batch * tiles_per_seq,),
        in_specs=[pl.BlockSpec((tm, D_MODEL), lambda i: (i, 0)),
                  _const_spec((D_MODEL, QKV_WIDTH))],
        out_specs=[pl.BlockSpec((None, d, tm // d, gw),
                                lambda i: (i // tiles_per_seq, 0, i % tiles_per_seq, 0))
                   for _, d in DIL_CONFIGS],
        out_shape=[jax.ShapeDtypeStruct((batch, d, seq_len // d, gw), jnp.bfloat16)
                   for _, d in DIL_CONFIGS],
        scratch_shapes=[pltpu.VMEM((D_MODEL // V7X_LANES, tm, V7X_LANES), jnp.float32)],
        compiler_params=_params(1),
        name="qkv_proj",
    )(x, w_qkv)


def _attn_bias(group):
    window, d = DIL_CONFIGS[group]
    qb, kb = ATTN_Q_BLOCK, ATTN_K_BLOCK
    r = (kb - qb) // 2
    a = np.arange(qb)[:, None]
    c = np.arange(kb)[None, :]
    rel = c - r - a
    band = np.abs(rel) <= r
    valid = np.stack([band, band & (c >= r), band & (c < qb + r)])
    neg_dist = np.where(valid, -(d * np.abs(rel)).astype(np.float64), MASK_VALUE).astype(np.float32)
    slopes = (_alibi_slopes()[group] * LOG2_E).astype(np.float32)
    return jnp.asarray(slopes)[None, :, None, None] * jnp.asarray(neg_dist)[:, None]


def _attn_body(n_blocks_total, q_ref, k_ref, kp_ref, kn_ref, v_ref, vp_ref, vn_ref, bias_ref,
               o_ref, stat_ref, ke_ref, ve_ref, s_ref, p_ref):
    tq = q_ref.shape[0]
    qb, kb = ATTN_Q_BLOCK, ATTN_K_BLOCK
    r = (kb - qb) // 2
    hq = qb // 2
    n_blk = tq // qb
    ti = pl.program_id(2)
    ke_ref[0:r, :] = kp_ref[...]
    ke_ref[r:r + tq, :] = k_ref[...]
    ke_ref[r + tq:, :] = kn_ref[...]
    ve_ref[0:r, :] = vp_ref[...]
    ve_ref[r:r + tq, :] = v_ref[...]
    ve_ref[r + tq:, :] = vn_ref[...]

    low_half = lax.broadcasted_iota(jnp.int32, (qb, V7X_LANES), 1) < HEAD_DIM
    lane = lax.broadcasted_iota(jnp.int32, (hq, STAT_WIDTH), 1)

    for blk in range(n_blk):
        row0 = blk * qb
        g_blk = ti * n_blk + blk
        variant = jnp.where(g_blk == 0, 1, 0) + jnp.where(g_blk == n_blocks_total - 1, 2, 0)
        stats = [jnp.zeros((hq, STAT_WIDTH), jnp.float32) for _ in range(2)]
        for pair in range(N_HEADS // 2):
            cols = slice(pair * V7X_LANES, (pair + 1) * V7X_LANES)
            buf = pair % 2
            q2 = q_ref[pl.ds(row0, qb), cols]
            zero = jnp.zeros_like(q2)
            q_stack = jnp.concatenate(
                [jnp.where(low_half, q2, zero), jnp.where(low_half, zero, q2)], axis=0)
            s_ref[buf] = lax.dot_general(q_stack, ke_ref[pl.ds(row0, kb), cols],
                                         (((1,), (1,)), ((), ())),
                                         preferred_element_type=jnp.float32)
            for c in range(4):
                head, half = 2 * pair + c // 2, c % 2
                rows = slice(c * hq, (c + 1) * hq)
                sc = s_ref[buf, rows, :] + bias_ref[variant, head, half * hq:(half + 1) * hq, :]
                m = jnp.max(sc, axis=1, keepdims=True)
                p = jnp.exp2(sc - m)
                l = jnp.sum(p, axis=1, keepdims=True)
                p_ref[buf, rows, :] = p.astype(jnp.bfloat16)
                stats[half] = jnp.where(lane == head, m, stats[half])
                stats[half] = jnp.where(lane == N_HEADS + head, l, stats[half])
            o2 = _dot(p_ref[buf], ve_ref[pl.ds(row0, kb), cols])
            o_ref[pl.ds(row0, qb), cols] = jnp.where(low_half, o2[:qb], o2[qb:]).astype(o_ref.dtype)
        for half in range(2):
            stat_ref[pl.ds(row0 + half * hq, hq), :] = stats[half]


def _attn_group(qkv_g, group):
    window, d = DIL_CONFIGS[group]
    batch, _, sub_len, _ = qkv_g.shape
    qb, kb = ATTN_Q_BLOCK, ATTN_K_BLOCK
    r = window // (2 * d)
    assert r == (kb - qb) // 2 and sub_len >= 2 * qb
    tq = min(ATTN_Q_TILE, sub_len)
    n_tiles = sub_len // tq
    halo_per_tile = tq // r
    n_halo_blocks = sub_len // r

    def main_spec(j, width=ATTN_WIDTH):
        return pl.BlockSpec((None, None, tq, width), lambda bi, ri, ti: (bi, ri, ti, j))

    def prev_spec(j):
        return pl.BlockSpec(
            (None, None, r, ATTN_WIDTH),
            lambda bi, ri, ti: (bi, ri, jnp.maximum(ti * halo_per_tile - 1, 0), j))

    def next_spec(j):
        return pl.BlockSpec(
            (None, None, r, ATTN_WIDTH),
            lambda bi, ri, ti: (bi, ri, jnp.minimum((ti + 1) * halo_per_tile, n_halo_blocks - 1), j))

    return pl.pallas_call(
        functools.partial(_attn_body, sub_len // qb),
        grid=(batch, d, n_tiles),
        in_specs=[main_spec(0),
                  main_spec(1), prev_spec(1), next_spec(1),
                  main_spec(2), prev_spec(2), next_spec(2),
                  _const_spec((3, N_HEADS, qb, kb))],
        out_specs=[main_spec(0), main_spec(0, STAT_WIDTH)],
        out_shape=[jax.ShapeDtypeStruct((batch, d, sub_len, ATTN_WIDTH), jnp.bfloat16),
                   jax.ShapeDtypeStruct((batch, d, sub_len, STAT_WIDTH), jnp.float32)],
        scratch_shapes=[pltpu.VMEM((tq + 2 * r, ATTN_WIDTH), jnp.bfloat16),
                        pltpu.VMEM((tq + 2 * r, ATTN_WIDTH), jnp.bfloat16),
                        pltpu.VMEM((2, 2 * qb, kb), jnp.float32),
                        pltpu.VMEM((2, 2 * qb, kb), jnp.bfloat16)],
        compiler_params=_params(3),
        name=f"attn_group{group}",
    )(*([qkv_g] * 7), _attn_bias(group))


def _head_expand_matrix():
    k = np.arange(2 * STAT_WIDTH)[:, None] % STAT_WIDTH
    c = np.arange(ATTN_WIDTH)[None, :] // HEAD_DIM
    return jnp.asarray(k == c, dtype=jnp.bfloat16)


def _attn_out_ln_body(x_ref, o0_ref, o1_ref, o2_ref, l0_ref, l1_ref, l2_ref, e_ref, w_ref,
                      gain_ref, bias_ref, out_ref, o_scr, l_scr):
    o_refs = (o0_ref, o1_ref, o2_ref)
    l_refs = (l0_ref, l1_ref, l2_ref)

    def token_order(ref, scr, g, r0, ts):
        d, _, width = ref.shape
        if d == 1:
            return ref[0, r0:r0 + ts, :].astype(jnp.float32)
        n_slabs = width // V7X_LANES
        for r in range(d):
            piece = ref[r, r0 // d:(r0 + ts) // d, :].astype(jnp.float32)
            for c in range(n_slabs):
                scr[g, c, pl.ds(r0 + r, ts // d, stride=d), :] = (
                    piece[:, c * V7X_LANES:(c + 1) * V7X_LANES])
        return jnp.concatenate([scr[g, c, r0:r0 + ts, :] for c in range(n_slabs)], axis=1)

    ts = OUT_SUB_TILE
    is_head = lax.broadcasted_iota(jnp.int32, (ts, STAT_WIDTH), 1) < N_HEADS
    for r0 in range(0, x_ref.shape[0], ts):
        ms = [token_order(l_refs[g], l_scr, g, r0, ts) for g in range(N_DIL_GROUPS)]
        ls = [pltpu.roll(m, STAT_WIDTH - N_HEADS, 1) for m in ms]
        top = jnp.maximum(jnp.maximum(ms[0], ms[1]), ms[2])
        es = [jnp.exp2(m - top) for m in ms]
        inv = 1.0 / (es[0] * ls[0] + es[1] * ls[1] + es[2] * ls[2])
        acc = None
        for g in range(N_DIL_GROUPS):
            wt = jnp.where(is_head, es[g] * inv, 0.0)
            hi = wt.astype(jnp.bfloat16)
            lo = (wt - hi.astype(jnp.float32)).astype(jnp.bfloat16)
            w_full = _dot(jnp.concatenate([hi, lo], axis=1), e_ref[...])
            term = w_full * token_order(o_refs[g], o_scr, g, r0, ts)
            acc = term if acc is None else acc + term
        m = _dot(acc.astype(jnp.bfloat16), w_ref[...])
        out_ref[r0:r0 + ts, :] = _layer_norm(ALPHA * x_ref[r0:r0 + ts, :] + m,
                                             gain_ref[...], bias_ref[...])


def _attn_out_ln(x, os_, stats, w_out, gain, bias, seq_len):
    m = x.shape[0]
    tm = OUT_ROW_TILE
    tiles_per_seq = seq_len // tm
    row_spec = pl.BlockSpec((tm, D_MODEL), lambda i: (i, 0))

    def group_spec(d, width):
        return pl.BlockSpec((None, d, tm // d, width),
                            lambda i: (i // tiles_per_seq, 0, i % tiles_per_seq, 0))

    return pl.pallas_call(
        _attn_out_ln_body,
        grid=(m // tm,),
        in_specs=[row_spec]
                 + [group_spec(d, ATTN_WIDTH) for _, d in DIL_CONFIGS]
                 + [group_spec(d, STAT_WIDTH) for _, d in DIL_CONFIGS]
                 + [_const_spec((2 * STAT_WIDTH, ATTN_WIDTH)),
                    _const_spec((ATTN_WIDTH, D_MODEL)),
                    _const_spec((1, D_MODEL)), _const_spec((1, D_MODEL))],
        out_specs=row_spec,
        out_shape=jax.ShapeDtypeStruct((m, D_MODEL), jnp.float32),
        scratch_shapes=[
            pltpu.VMEM((N_DIL_GROUPS, ATTN_WIDTH // V7X_LANES, tm, V7X_LANES), jnp.float32),
            pltpu.VMEM((N_DIL_GROUPS, STAT_WIDTH // V7X_LANES, tm, V7X_LANES), jnp.float32)],
        compiler_params=_params(1),
        name="attn_out_ln",
    )(x, *os_, *stats, _head_expand_matrix(), w_out, gain, bias)


def kernel(x, ffn1_w_gate, ffn1_w_up, ffn1_w_down, ffn2_w_gate, ffn2_w_up, ffn2_w_down,
           ln_gain, ln_bias, pool_w_in, pool_w_group, pool_scale, pool_w_out,
           attn_w_qkv, attn_w_out):
    batch, seq_len, _ = x.shape
    bf16 = jnp.bfloat16
    n_mixers = 2
    h = x.reshape(batch * seq_len, D_MODEL)

    def ln_params(i, j):
        return ln_gain[i, j].reshape(1, D_MODEL), ln_bias[i, j].reshape(1, D_MODEL)

    ffn1 = [w.astype(bf16) for w in (ffn1_w_gate, ffn1_w_up, ffn1_w_down)]
    ffn2 = [w.astype(bf16) for w in (ffn2_w_gate, ffn2_w_up, ffn2_w_down)]
    for i in range(DEPTH):
        h = _ffn_ln(h, *ffn1, *ln_params(i, 0), i)
        if i % n_mixers == 0:
            p = i // n_mixers
            h = _pool_ln(h.reshape(batch, seq_len, D_MODEL), pool_w_in[p].astype(bf16),
                         pool_w_group[p].astype(bf16), pool_scale[p].reshape(1, D_MODEL),
                         pool_w_out[p].astype(bf16), *ln_params(i, 1))
            h = h.reshape(batch * seq_len, D_MODEL)
        else:
            a = i // n_mixers
            qkv = _qkv_proj(h, attn_w_qkv[a].astype(bf16), batch, seq_len)
            outs = [_attn_group(qkv[g], g) for g in range(N_DIL_GROUPS)]
            h = _attn_out_ln(h, [o for o, _ in outs], [l for _, l in outs],
                             attn_w_out[a].astype(bf16), *ln_params(i, 1), seq_len)
        h = _ffn_ln(h, *ffn2, *ln_params(i, 2), i)
    return h.reshape(batch, seq_len, D_MODEL)
```

```python
import functools

import numpy as np
import jax
import jax.numpy as jnp
from jax import lax
from jax.experimental import pallas as pl
from jax.experimental.pallas import tpu as pltpu

D_MODEL = 1024
DEPTH = 2
POOL_WINDOWS = (2, 4, 8, 16)
N_POOL_GROUPS = len(POOL_WINDOWS)
POOL_GROUP_DIM = D_MODEL // N_POOL_GROUPS
POOL_HALO = max(POOL_WINDOWS) // 2
HEAD_DIM = 64
N_HEADS = D_MODEL // HEAD_DIM
DIL_CONFIGS = ((128, 1), (512, 4), (2048, 16))
N_DIL_GROUPS = len(DIL_CONFIGS)
ATTN_WIDTH = N_HEADS * HEAD_DIM
QKV_WIDTH = N_DIL_GROUPS * 3 * ATTN_WIDTH
D_FF = 2816
MACARON_WEIGHT = 0.5
ALPHA = (2.0 * DEPTH) ** 0.25
LN_EPS = 1e-5
MASK_VALUE = -1e30
LOG2_E = float(np.log2(np.e))

V7X_LANES = 128
V7X_MXU_DIM = 256
V7X_SUBLANES = 8
V7X_VMEM_LIMIT_BYTES = 56 * 1024 * 1024

FFN_ROW_TILE = 1024
FFN_SUB_TILE = 256
FFN_CHUNKS = ((0, 6 * V7X_MXU_DIM), (6 * V7X_MXU_DIM, D_FF))
WEIGHT_LOAD_CHUNKS = 8
POOL_ROW_TILE = 1024
POOL_SUB_TILE = 256
QKV_ROW_TILE = 512
QKV_WEIGHT_LOAD_CHUNKS = 32
ATTN_Q_TILE = 512
ATTN_Q_BLOCK = 128
ATTN_K_BLOCK = 256
OUT_ROW_TILE = 1024
OUT_SUB_TILE = 256
STAT_WIDTH = V7X_LANES


def _alibi_slopes():
    n = N_DIL_GROUPS * N_HEADS
    s = 2.0 ** (-8.0 * np.arange(1, n + 1) / n)
    return s.reshape(N_DIL_GROUPS, N_HEADS)


def _layer_norm(y, gain, bias):
    mu = jnp.mean(y, axis=-1, keepdims=True)
    yc = y - mu
    var = jnp.mean(yc * yc, axis=-1, keepdims=True)
    return yc * lax.rsqrt(var + LN_EPS) * gain + bias


def _dot(a, b):
    return jnp.dot(a, b, preferred_element_type=jnp.float32)


def _const_spec(shape):
    return pl.BlockSpec(shape, lambda *_: (0,) * len(shape), pipeline_mode=pl.Buffered(1))


def _params(n_grid_dims):
    return pltpu.CompilerParams(
        dimension_semantics=("arbitrary",) * n_grid_dims,
        vmem_limit_bytes=V7X_VMEM_LIMIT_BYTES)


def _load_weight_bf16(src_hbm, dst_ref, stage_ref, sem):
    n_rows = stage_ref.shape[1]
    n_chunks = src_hbm.shape[0] // n_rows
    assert n_chunks * n_rows == src_hbm.shape[0]

    def chunk_copy(c):
        return pltpu.make_async_copy(src_hbm.at[pl.ds(c * n_rows, n_rows), :],
                                     stage_ref.at[c % 2], sem.at[c % 2])

    chunk_copy(0).start()
    for c in range(n_chunks):
        if c + 1 < n_chunks:
            chunk_copy(c + 1).start()
        chunk_copy(c).wait()
        dst_ref[c * n_rows:(c + 1) * n_rows, :] = stage_ref[c % 2].astype(jnp.bfloat16)


def _ffn_ln_body(layer, x_ref, wg_hbm, wu_hbm, wd_hbm, gain_ref, bias_ref, o_ref,
                 wg_ref, wu_ref, wd_ref, stage_in_ref, stage_down_ref, sem):
    @pl.when(pl.program_id(0) == 0)
    def _():
        _load_weight_bf16(wg_hbm.at[layer], wg_ref, stage_in_ref, sem)
        _load_weight_bf16(wu_hbm.at[layer], wu_ref, stage_in_ref, sem)
        _load_weight_bf16(wd_hbm.at[layer], wd_ref, stage_down_ref, sem)

    for r0 in range(0, x_ref.shape[0], FFN_SUB_TILE):
        rows = slice(r0, r0 + FFN_SUB_TILE)
        x = x_ref[rows, :]
        xb = x.astype(jnp.bfloat16)
        h = None
        for lo, hi in FFN_CHUNKS:
            g = _dot(xb, wg_ref[:, lo:hi])
            u = _dot(xb, wu_ref[:, lo:hi])
            hg = 0.5 * g
            a = ((hg + hg * jnp.tanh(hg)) * u).astype(jnp.bfloat16)
            part = _dot(a, wd_ref[lo:hi, :])
            h = part if h is None else h + part
        y = ALPHA * x + MACARON_WEIGHT * h
        o_ref[rows, :] = _layer_norm(y, gain_ref[...], bias_ref[...])


def _ffn_ln(x, w_gate, w_up, w_down, gain, bias, layer):
    m = x.shape[0]
    tm = FFN_ROW_TILE
    row_spec = pl.BlockSpec((tm, D_MODEL), lambda i: (i, 0))
    hbm_spec = pl.BlockSpec(memory_space=pl.ANY)
    return pl.pallas_call(
        functools.partial(_ffn_ln_body, layer),
        grid=(m // tm,),
        in_specs=[row_spec, hbm_spec, hbm_spec, hbm_spec,
                  _const_spec((1, D_MODEL)), _const_spec((1, D_MODEL))],
        out_specs=row_spec,
        out_shape=jax.ShapeDtypeStruct((m, D_MODEL), jnp.float32),
        scratch_shapes=[pltpu.VMEM((D_MODEL, D_FF), jnp.bfloat16),
                        pltpu.VMEM((D_MODEL, D_FF), jnp.bfloat16),
                        pltpu.VMEM((D_FF, D_MODEL), jnp.bfloat16),
                        pltpu.VMEM((2, D_MODEL // WEIGHT_LOAD_CHUNKS, D_FF), jnp.float32),
                        pltpu.VMEM((2, D_FF // WEIGHT_LOAD_CHUNKS, D_MODEL), jnp.float32),
                        pltpu.SemaphoreType.DMA((2,))],
        compiler_params=_params(1),
        name="ffn_ln",
    )(x, w_gate, w_up, w_down, gain, bias)


def _pool_ln_body(seq_len, xm_ref, xp_ref, xn_ref, win_ref, wgrp_ref, scale_ref, wout_ref,
                  gain_ref, bias_ref, o_ref, xe_ref, u_ref, lvl_ref):
    ts = xm_ref.shape[0]
    halo = POOL_HALO
    s = pl.program_id(1)
    n_s = pl.num_programs(1)
    xe_ref[0:halo, :] = xp_ref[...]
    xe_ref[halo:halo + ts, :] = xm_ref[...]
    xe_ref[halo + ts:, :] = xn_ref[...]
    tb = POOL_SUB_TILE
    n_ext = tb + 2 * halo
    n_sub = ts // tb
    for k in range(n_sub):
        r0 = k * tb
        u_ext = _dot(xe_ref[r0:r0 + n_ext, :].astype(jnp.bfloat16), win_ref[...])
        row = lax.broadcasted_iota(jnp.int32, (n_ext, 1), 0)
        if k == 0:
            u_ext = jnp.where((row < halo) & (s == 0), 0.0, u_ext)
        if k == n_sub - 1:
            u_ext = jnp.where((row >= halo + tb) & (s == n_s - 1), 0.0, u_ext)
        u_ref[k, 0:n_ext, :] = u_ext
        u_ref[k, n_ext:, :] = jnp.zeros((V7X_SUBLANES, D_MODEL), jnp.float32)

        t_abs = s * ts + r0 + lax.broadcasted_iota(jnp.int32, (tb, 1), 0)
        ys = []
        lvl = 0
        for g, w in enumerate(POOL_WINDOWS):
            half = w // 2
            cols = slice(g * POOL_GROUP_DIM, (g + 1) * POOL_GROUP_DIM)
            src = functools.partial(lambda k_, c_, lo, n: u_ref[k_, lo:lo + n, c_], k, cols)
            width = 1
            while 2 * width < w:
                lvl_ref[k, lvl, 0:n_ext, :] = src(0, n_ext) + src(width, n_ext)
                lvl_ref[k, lvl, n_ext:, :] = jnp.zeros((V7X_SUBLANES, POOL_GROUP_DIM), jnp.float32)
                src = functools.partial(lambda k_, l_, lo, n: lvl_ref[k_, l_, lo:lo + n, :], k, lvl)
                width, lvl = 2 * width, lvl + 1
            start = halo - half
            win_sum = src(start, tb) + src(start + width, tb)
            count = jnp.minimum(t_abs + half, seq_len) - jnp.maximum(t_abs - half, 0)
            mean = win_sum / count.astype(jnp.float32)
            mixed = (mean - u_ref[k, halo:halo + tb, cols]).astype(jnp.bfloat16)
            ys.append(_dot(mixed, wgrp_ref[g]))
        y = (jnp.concatenate(ys, axis=1) * scale_ref[...]).astype(jnp.bfloat16)
        m = _dot(y, wout_ref[...])
        o_ref[r0:r0 + tb, :] = _layer_norm(ALPHA * xm_ref[r0:r0 + tb, :] + m,
                                           gain_ref[...], bias_ref[...])


def _pool_ln(x, w_in, w_group, scale, w_out, gain, bias):
    b, s_len, _ = x.shape
    ts = POOL_ROW_TILE
    halo = POOL_HALO
    blocks_per_tile = ts // halo
    n_halo_blocks = s_len // halo
    n_levels = sum(int(np.log2(w)) - 1 for w in POOL_WINDOWS)
    tb = POOL_SUB_TILE
    buf_rows = tb + 2 * halo + V7X_SUBLANES
    main_spec = pl.BlockSpec((None, ts, D_MODEL), lambda bi, si: (bi, si, 0))
    prev_spec = pl.BlockSpec(
        (None, halo, D_MODEL),
        lambda bi, si: (bi, jnp.maximum(si * blocks_per_tile - 1, 0), 0))
    next_spec = pl.BlockSpec(
        (None, halo, D_MODEL),
        lambda bi, si: (bi, jnp.minimum((si + 1) * blocks_per_tile, n_halo_blocks - 1), 0))
    return pl.pallas_call(
        functools.partial(_pool_ln_body, s_len),
        grid=(b, s_len // ts),
        in_specs=[main_spec, prev_spec, next_spec,
                  _const_spec((D_MODEL, D_MODEL)),
                  _const_spec((N_POOL_GROUPS, POOL_GROUP_DIM, POOL_GROUP_DIM)),
                  _const_spec((1, D_MODEL)),
                  _const_spec((D_MODEL, D_MODEL)),
                  _const_spec((1, D_MODEL)), _const_spec((1, D_MODEL))],
        out_specs=main_spec,
        out_shape=jax.ShapeDtypeStruct(x.shape, jnp.float32),
        scratch_shapes=[pltpu.VMEM((ts + 2 * halo, D_MODEL), jnp.float32),
                        pltpu.VMEM((ts // tb, buf_rows, D_MODEL), jnp.float32),
                        pltpu.VMEM((ts // tb, n_levels, buf_rows, POOL_GROUP_DIM), jnp.float32)],
        compiler_params=_params(2),
        name="pool_ln",
    )(x, x, x, w_in, w_group, scale, w_out, gain, bias)


def _qkv_body(x_ref, w_hbm, o0_ref, o1_ref, o2_ref, x_slabs, w_ref, stage_ref, sem):
    @pl.when(pl.program_id(0) == 0)
    def _():
        _load_weight_bf16(w_hbm, w_ref, stage_ref, sem)

    tm = x_ref.shape[0]
    n_slabs = D_MODEL // V7X_LANES
    for c in range(n_slabs):
        x_slabs[c] = x_ref[:, c * V7X_LANES:(c + 1) * V7X_LANES]
    for g, ((_, d), o_ref) in enumerate(zip(DIL_CONFIGS, (o0_ref, o1_ref, o2_ref))):
        n = tm // d
        if d == 1:
            xp = x_ref[...]
        else:
            xp = jnp.concatenate(
                [jnp.concatenate([x_slabs[c, pl.ds(r, n, stride=d), :] for r in range(d)], axis=0)
                 for c in range(n_slabs)], axis=1)
        xp = xp.astype(jnp.bfloat16)
        for j in range(3):
            lo = (g * 3 + j) * ATTN_WIDTH
            res = _dot(xp, w_ref[:, lo:lo + ATTN_WIDTH])
            if j == 0:
                res = res * (HEAD_DIM ** -0.5 * LOG2_E)
            res = res.astype(jnp.bfloat16)
            for r in range(d):
                o_ref[r, :, j * ATTN_WIDTH:(j + 1) * ATTN_WIDTH] = res[r * n:(r + 1) * n]


def _qkv_proj(x, w_qkv, batch, seq_len):
    tm = QKV_ROW_TILE
    tiles_per_seq = seq_len // tm
    gw = 3 * ATTN_WIDTH
    return pl.pallas_call(
        _qkv_body,
        grid=(batch * tiles_per_seq,),
        in_specs=[pl.BlockSpec((tm, D_MODEL), lambda i: (i, 0)),
                  pl.BlockSpec(memory_space=pl.ANY)],
        out_specs=[pl.BlockSpec((None, d, tm // d, gw),
                                lambda i: (i // tiles_per_seq, 0, i % tiles_per_seq, 0))
                   for _, d in DIL_CONFIGS],
        out_shape=[jax.ShapeDtypeStruct((batch, d, seq_len // d, gw), jnp.bfloat16)
                   for _, d in DIL_CONFIGS],
        scratch_shapes=[pltpu.VMEM((D_MODEL // V7X_LANES, tm, V7X_LANES), jnp.float32),
                        pltpu.VMEM((D_MODEL, QKV_WIDTH), jnp.bfloat16),
                        pltpu.VMEM((2, D_MODEL // QKV_WEIGHT_LOAD_CHUNKS, QKV_WIDTH), jnp.float32),
                        pltpu.SemaphoreType.DMA((2,))],
        compiler_params=_params(1),
        name="qkv_proj",
    )(x, w_qkv)


def _attn_bias(group):
    window, d = DIL_CONFIGS[group]
    qb, kb = ATTN_Q_BLOCK, ATTN_K_BLOCK
    r = (kb - qb) // 2
    a = np.arange(qb)[:, None]
    c = np.arange(kb)[None, :]
    rel = c - r - a
    band = np.abs(rel) <= r
    valid = np.stack([band, band & (c >= r), band & (c < qb + r)])
    neg_dist = np.where(valid, -(d * np.abs(rel)).astype(np.float64), MASK_VALUE).astype(np.float32)
    slopes = (_alibi_slopes()[group] * LOG2_E).astype(np.float32)
    return jnp.asarray(slopes)[None, :, None, None] * jnp.asarray(neg_dist)[:, None]


def _attn_body(n_blocks_total, q_ref, k_ref, kp_ref, kn_ref, v_ref, vp_ref, vn_ref, bias_ref,
               o_ref, stat_ref, ke_ref, ve_ref, s_ref, p_ref):
    n_res, tq, _ = q_ref.shape
    qb, kb = ATTN_Q_BLOCK, ATTN_K_BLOCK
    r = (kb - qb) // 2
    hq = qb // 2
    n_blk = tq // qb
    ti = pl.program_id(2)
    ke_ref[:, 0:r, :] = kp_ref[...]
    ke_ref[:, r:r + tq, :] = k_ref[...]
    ke_ref[:, r + tq:, :] = kn_ref[...]
    ve_ref[:, 0:r, :] = vp_ref[...]
    ve_ref[:, r:r + tq, :] = v_ref[...]
    ve_ref[:, r + tq:, :] = vn_ref[...]

    low_half = lax.broadcasted_iota(jnp.int32, (qb, V7X_LANES), 1) < HEAD_DIM
    lane = lax.broadcasted_iota(jnp.int32, (hq, STAT_WIDTH), 1)

    for res, blk in [(a, b) for a in range(n_res) for b in range(n_blk)]:
        row0 = blk * qb
        g_blk = ti * n_blk + blk
        variant = jnp.where(g_blk == 0, 1, 0) + jnp.where(g_blk == n_blocks_total - 1, 2, 0)
        stats = [jnp.zeros((hq, STAT_WIDTH), jnp.float32) for _ in range(2)]
        for pair in range(N_HEADS // 2):
            cols = slice(pair * V7X_LANES, (pair + 1) * V7X_LANES)
            buf = pair % 2
            q2 = q_ref[res, pl.ds(row0, qb), cols]
            zero = jnp.zeros_like(q2)
            q_stack = jnp.concatenate(
                [jnp.where(low_half, q2, zero), jnp.where(low_half, zero, q2)], axis=0)
            s_ref[buf] = lax.dot_general(q_stack, ke_ref[res, pl.ds(row0, kb), cols],
                                         (((1,), (1,)), ((), ())),
                                         preferred_element_type=jnp.float32)
            for c in range(4):
                head, half = 2 * pair + c // 2, c % 2
                rows = slice(c * hq, (c + 1) * hq)
                sc = s_ref[buf, rows, :] + bias_ref[variant, head, half * hq:(half + 1) * hq, :]
                m = jnp.max(sc, axis=1, keepdims=True)
                p = jnp.exp2(sc - m)
                l = jnp.sum(p, axis=1, keepdims=True)
                p_ref[buf, rows, :] = p.astype(jnp.bfloat16)
                stats[half] = jnp.where(lane == head, m, stats[half])
                stats[half] = jnp.where(lane == N_HEADS + head, l, stats[half])
            o2 = _dot(p_ref[buf], ve_ref[res, pl.ds(row0, kb), cols])
            o_ref[res, pl.ds(row0, qb), cols] = jnp.where(
                low_half, o2[:qb], o2[qb:]).astype(o_ref.dtype)
        for half in range(2):
            stat_ref[res, pl.ds(row0 + half * hq, hq), :] = stats[half]


def _attn_group(qkv_g, group):
    window, d = DIL_CONFIGS[group]
    batch, _, sub_len, _ = qkv_g.shape
    qb, kb = ATTN_Q_BLOCK, ATTN_K_BLOCK
    r = window // (2 * d)
    assert r == (kb - qb) // 2 and sub_len >= 2 * qb
    tq = min(ATTN_Q_TILE, sub_len)
    n_tiles = sub_len // tq
    n_res = min(d, ATTN_Q_TILE // tq)
    halo_per_tile = tq // r
    n_halo_blocks = sub_len // r

    def main_spec(j, width=ATTN_WIDTH):
        return pl.BlockSpec((None, n_res, tq, width), lambda bi, ri, ti: (bi, ri, ti, j))

    def prev_spec(j):
        return pl.BlockSpec(
            (None, n_res, r, ATTN_WIDTH),
            lambda bi, ri, ti: (bi, ri, jnp.maximum(ti * halo_per_tile - 1, 0), j))

    def next_spec(j):
        return pl.BlockSpec(
            (None, n_res, r, ATTN_WIDTH),
            lambda bi, ri, ti: (bi, ri, jnp.minimum((ti + 1) * halo_per_tile, n_halo_blocks - 1), j))

    return pl.pallas_call(
        functools.partial(_attn_body, sub_len // qb),
        grid=(batch, d // n_res, n_tiles),
        in_specs=[main_spec(0),
                  main_spec(1), prev_spec(1), next_spec(1),
                  main_spec(2), prev_spec(2), next_spec(2),
                  _const_spec((3, N_HEADS, qb, kb))],
        out_specs=[main_spec(0), main_spec(0, STAT_WIDTH)],
        out_shape=[jax.ShapeDtypeStruct((batch, d, sub_len, ATTN_WIDTH), jnp.bfloat16),
                   jax.ShapeDtypeStruct((batch, d, sub_len, STAT_WIDTH), jnp.float32)],
        scratch_shapes=[pltpu.VMEM((n_res, tq + 2 * r, ATTN_WIDTH), jnp.bfloat16),
                        pltpu.VMEM((n_res, tq + 2 * r, ATTN_WIDTH), jnp.bfloat16),
                        pltpu.VMEM((2, 2 * qb, kb), jnp.float32),
                        pltpu.VMEM((2, 2 * qb, kb), jnp.bfloat16)],
        compiler_params=_params(3),
        name=f"attn_group{group}",
    )(*([qkv_g] * 7), _attn_bias(group))


def _head_expand_matrix():
    k = np.arange(2 * STAT_WIDTH)[:, None] % STAT_WIDTH
    c = np.arange(ATTN_WIDTH)[None, :] // HEAD_DIM
    return jnp.asarray(k == c, dtype=jnp.bfloat16)


def _attn_out_ln_body(x_ref, o0_ref, o1_ref, o2_ref, l0_ref, l1_ref, l2_ref, e_ref, w_ref,
                      gain_ref, bias_ref, out_ref, o_scr, l_scr):
    o_refs = (o0_ref, o1_ref, o2_ref)
    l_refs = (l0_ref, l1_ref, l2_ref)

    def token_order(ref, scr, g, r0, ts):
        d, _, width = ref.shape
        if d == 1:
            return ref[0, r0:r0 + ts, :].astype(jnp.float32)
        n_slabs = width // V7X_LANES
        for r in range(d):
            piece = ref[r, r0 // d:(r0 + ts) // d, :].astype(jnp.float32)
            for c in range(n_slabs):
                scr[g, c, pl.ds(r0 + r, ts // d, stride=d), :] = (
                    piece[:, c * V7X_LANES:(c + 1) * V7X_LANES])
        return jnp.concatenate([scr[g, c, r0:r0 + ts, :] for c in range(n_slabs)], axis=1)

    ts = OUT_SUB_TILE
    is_head = lax.broadcasted_iota(jnp.int32, (ts, STAT_WIDTH), 1) < N_HEADS
    for r0 in range(0, x_ref.shape[0], ts):
        ms = [token_order(l_refs[g], l_scr, g, r0, ts) for g in range(N_DIL_GROUPS)]
        ls = [pltpu.roll(m, STAT_WIDTH - N_HEADS, 1) for m in ms]
        top = jnp.maximum(jnp.maximum(ms[0], ms[1]), ms[2])
        es = [jnp.exp2(m - top) for m in ms]
        inv = 1.0 / (es[0] * ls[0] + es[1] * ls[1] + es[2] * ls[2])
        acc = None
        for g in range(N_DIL_GROUPS):
            wt = jnp.where(is_head, es[g] * inv, 0.0)
            hi = wt.astype(jnp.bfloat16)
            lo = (wt - hi.astype(jnp.float32)).astype(jnp.bfloat16)
            w_full = _dot(jnp.concatenate([hi, lo], axis=1), e_ref[...])
            term = w_full * token_order(o_refs[g], o_scr, g, r0, ts)
            acc = term if acc is None else acc + term
        m = _dot(acc.astype(jnp.bfloat16), w_ref[...])
        out_ref[r0:r0 + ts, :] = _layer_norm(ALPHA * x_ref[r0:r0 + ts, :] + m,
                                             gain_ref[...], bias_ref[...])


def _attn_out_ln(x, os_, stats, w_out, gain, bias, seq_len):
    m = x.shape[0]
    tm = OUT_ROW_TILE
    tiles_per_seq = seq_len // tm
    row_spec = pl.BlockSpec((tm, D_MODEL), lambda i: (i, 0))

    def group_spec(d, width):
        return pl.BlockSpec((None, d, tm // d, width),
                            lambda i: (i // tiles_per_seq, 0, i % tiles_per_seq, 0))

    return pl.pallas_call(
        _attn_out_ln_body,
        grid=(m // tm,),
        in_specs=[row_spec]
                 + [group_spec(d, ATTN_WIDTH) for _, d in DIL_CONFIGS]
                 + [group_spec(d, STAT_WIDTH) for _, d in DIL_CONFIGS]
                 + [_const_spec((2 * STAT_WIDTH, ATTN_WIDTH)),
                    _const_spec((ATTN_WIDTH, D_MODEL)),
                    _const_spec((1, D_MODEL)), _const_spec((1, D_MODEL))],
        out_specs=row_spec,
        out_shape=jax.ShapeDtypeStruct((m, D_MODEL), jnp.float32),
        scratch_shapes=[
            pltpu.VMEM((N_DIL_GROUPS, ATTN_WIDTH // V7X_LANES, tm, V7X_LANES), jnp.float32),
            pltpu.VMEM((N_DIL_GROUPS, STAT_WIDTH // V7X_LANES, tm, V7X_LANES), jnp.float32)],
        compiler_params=_params(1),
        name="attn_out_ln",
    )(x, *os_, *stats, _head_expand_matrix(), w_out, gain, bias)


def kernel(x, ffn1_w_gate, ffn1_w_up, ffn1_w_down, ffn2_w_gate, ffn2_w_up, ffn2_w_down,
           ln_gain, ln_bias, pool_w_in, pool_w_group, pool_scale, pool_w_out,
           attn_w_qkv, attn_w_out):
    batch, seq_len, _ = x.shape
    bf16 = jnp.bfloat16
    n_mixers = 2
    h = x.reshape(batch * seq_len, D_MODEL)

    def ln_params(i, j):
        return ln_gain[i, j].reshape(1, D_MODEL), ln_bias[i, j].reshape(1, D_MODEL)

    ffn1 = (ffn1_w_gate, ffn1_w_up, ffn1_w_down)
    ffn2 = (ffn2_w_gate, ffn2_w_up, ffn2_w_down)
    for i in range(DEPTH):
        h = _ffn_ln(h, *ffn1, *ln_params(i, 0), i)
        if i % n_mixers == 0:
            p = i // n_mixers
            h = _pool_ln(h.reshape(batch, seq_len, D_MODEL), pool_w_in[p].astype(bf16),
                         pool_w_group[p].astype(bf16), pool_scale[p].reshape(1, D_MODEL),
                         pool_w_out[p].astype(bf16), *ln_params(i, 1))
            h = h.reshape(batch * seq_len, D_MODEL)
        else:
            a = i // n_mixers
            qkv = _qkv_proj(h, attn_w_qkv[a], batch, seq_len)
            outs = [_attn_group(qkv[g], g) for g in range(N_DIL_GROUPS)]
            h = _attn_out_ln(h, [o for o, _ in outs], [l for _, l in outs],
                             attn_w_out[a].astype(bf16), *ln_params(i, 1), seq_len)
        h = _ffn_ln(h, *ffn2, *ln_params(i, 2), i)
    return h.reshape(batch, seq_len, D_MODEL)
```

```python
import functools

import numpy as np
import jax
import jax.numpy as jnp
from jax import lax
from jax.experimental import pallas as pl
from jax.experimental.pallas import tpu as pltpu

D_MODEL = 1024
DEPTH = 2
POOL_WINDOWS = (2, 4, 8, 16)
N_POOL_GROUPS = len(POOL_WINDOWS)
POOL_GROUP_DIM = D_MODEL // N_POOL_GROUPS
POOL_HALO = max(POOL_WINDOWS) // 2
HEAD_DIM = 64
N_HEADS = D_MODEL // HEAD_DIM
DIL_CONFIGS = ((128, 1), (512, 4), (2048, 16))
N_DIL_GROUPS = len(DIL_CONFIGS)
ATTN_WIDTH = N_HEADS * HEAD_DIM
QKV_WIDTH = N_DIL_GROUPS * 3 * ATTN_WIDTH
D_FF = 2816
MACARON_WEIGHT = 0.5
ALPHA = (2.0 * DEPTH) ** 0.25
LN_EPS = 1e-5
MASK_VALUE = -1e30
LOG2_E = float(np.log2(np.e))

V7X_LANES = 128
V7X_MXU_DIM = 256
V7X_SUBLANES = 8
V7X_BF16_SUBLANES = 16
V7X_VMEM_LIMIT_BYTES = 56 * 1024 * 1024

FFN_ROW_TILE = 1024
FFN_SUB_TILE = 256
FFN_CHUNKS = ((0, 6 * V7X_MXU_DIM), (6 * V7X_MXU_DIM, D_FF))
WEIGHT_LOAD_CHUNKS = 8
POOL_ROW_TILE = 1024
POOL_SUB_TILE = 256
QKV_ROW_TILE = 512
ATTN_Q_TILE = 512
ATTN_Q_BLOCK = 128
ATTN_K_BLOCK = 256
OUT_ROW_TILE = 1024
OUT_SUB_TILE = 256
STAT_WIDTH = V7X_LANES


def _alibi_slopes():
    n = N_DIL_GROUPS * N_HEADS
    s = 2.0 ** (-8.0 * np.arange(1, n + 1) / n)
    return s.reshape(N_DIL_GROUPS, N_HEADS)


def _layer_norm(y, gain, bias):
    mu = jnp.mean(y, axis=-1, keepdims=True)
    yc = y - mu
    var = jnp.mean(yc * yc, axis=-1, keepdims=True)
    return yc * lax.rsqrt(var + LN_EPS) * gain + bias


def _dot(a, b):
    return jnp.dot(a, b, preferred_element_type=jnp.float32)


def _const_spec(shape):
    return pl.BlockSpec(shape, lambda *_: (0,) * len(shape), pipeline_mode=pl.Buffered(1))


def _params(n_grid_dims):
    return pltpu.CompilerParams(
        dimension_semantics=("arbitrary",) * n_grid_dims,
        vmem_limit_bytes=V7X_VMEM_LIMIT_BYTES)


def _load_weight_bf16(src_hbm, dst_ref, stage_ref, sem):
    n_rows = stage_ref.shape[1]
    n_chunks = src_hbm.shape[0] // n_rows
    assert n_chunks * n_rows == src_hbm.shape[0]

    def chunk_copy(c):
        return pltpu.make_async_copy(src_hbm.at[pl.ds(c * n_rows, n_rows), :],
                                     stage_ref.at[c % 2], sem.at[c % 2])

    chunk_copy(0).start()
    for c in range(n_chunks):
        if c + 1 < n_chunks:
            chunk_copy(c + 1).start()
        chunk_copy(c).wait()
        dst_ref[c * n_rows:(c + 1) * n_rows, :] = stage_ref[c % 2].astype(jnp.bfloat16)


def _convert_specs(convert, n_steps, step_index):
    in_specs, out_specs, out_shapes = [], [], []
    for stack, layer in convert:
        _, rows, cols = stack.shape
        blk = rows // n_steps
        assert blk * n_steps == rows and blk % V7X_BF16_SUBLANES == 0
        in_specs.append(pl.BlockSpec(
            (None, blk, cols), lambda *ids, layer=layer: (layer, step_index(*ids), 0)))
        out_specs.append(pl.BlockSpec((blk, cols), lambda *ids: (step_index(*ids), 0)))
        out_shapes.append(jax.ShapeDtypeStruct((rows, cols), jnp.bfloat16))
    return in_specs, out_specs, out_shapes


def _ffn_ln_body(layer, n_conv, *refs):
    x_ref, wg_in, wu_in, wd_in, gain_ref, bias_ref = refs[:6]
    conv_in = refs[6:6 + n_conv]
    o_ref = refs[6 + n_conv]
    conv_out = refs[7 + n_conv:7 + 2 * n_conv]
    scratch = refs[7 + 2 * n_conv:]
    if layer is None:
        wg_ref, wu_ref, wd_ref = wg_in, wu_in, wd_in
    else:
        wg_ref, wu_ref, wd_ref, stage_in_ref, stage_down_ref, sem = scratch

        @pl.when(pl.program_id(0) == 0)
        def _():
            _load_weight_bf16(wg_in.at[layer], wg_ref, stage_in_ref, sem)
            _load_weight_bf16(wu_in.at[layer], wu_ref, stage_in_ref, sem)
            _load_weight_bf16(wd_in.at[layer], wd_ref, stage_down_ref, sem)

    for src, dst in zip(conv_in, conv_out):
        dst[...] = src[...].astype(jnp.bfloat16)

    for r0 in range(0, x_ref.shape[0], FFN_SUB_TILE):
        rows = slice(r0, r0 + FFN_SUB_TILE)
        x = x_ref[rows, :]
        xb = x.astype(jnp.bfloat16)
        h = None
        for lo, hi in FFN_CHUNKS:
            g = _dot(xb, wg_ref[:, lo:hi])
            u = _dot(xb, wu_ref[:, lo:hi])
            hg = 0.5 * g
            a = ((hg + hg * jnp.tanh(hg)) * u).astype(jnp.bfloat16)
            part = _dot(a, wd_ref[lo:hi, :])
            h = part if h is None else h + part
        y = ALPHA * x + MACARON_WEIGHT * h
        o_ref[rows, :] = _layer_norm(y, gain_ref[...], bias_ref[...])


def _ffn_ln(x, w_gate, w_up, w_down, gain, bias, layer=None, convert=()):
    m = x.shape[0]
    tm = FFN_ROW_TILE
    n_steps = m // tm
    row_spec = pl.BlockSpec((tm, D_MODEL), lambda i: (i, 0))
    if layer is None:
        weight_specs = [_const_spec((D_MODEL, D_FF)), _const_spec((D_MODEL, D_FF)),
                        _const_spec((D_FF, D_MODEL))]
        scratch = []
    else:
        weight_specs = [pl.BlockSpec(memory_space=pl.ANY)] * 3
        scratch = [pltpu.VMEM((D_MODEL, D_FF), jnp.bfloat16),
                   pltpu.VMEM((D_MODEL, D_FF), jnp.bfloat16),
                   pltpu.VMEM((D_FF, D_MODEL), jnp.bfloat16),
                   pltpu.VMEM((2, D_MODEL // WEIGHT_LOAD_CHUNKS, D_FF), jnp.float32),
                   pltpu.VMEM((2, D_FF // WEIGHT_LOAD_CHUNKS, D_MODEL), jnp.float32),
                   pltpu.SemaphoreType.DMA((2,))]
    conv_in, conv_out, conv_shapes = _convert_specs(convert, n_steps, lambda i: i)
    outs = pl.pallas_call(
        functools.partial(_ffn_ln_body, layer, len(convert)),
        grid=(n_steps,),
        in_specs=[row_spec] + weight_specs
                 + [_const_spec((1, D_MODEL)), _const_spec((1, D_MODEL))] + conv_in,
        out_specs=[row_spec] + conv_out,
        out_shape=[jax.ShapeDtypeStruct((m, D_MODEL), jnp.float32)] + conv_shapes,
        scratch_shapes=scratch,
        compiler_params=_params(1),
        name="ffn_ln",
    )(x, w_gate, w_up, w_down, gain, bias, *[stack for stack, _ in convert])
    return outs[0], outs[1:]


def _pool_ln_body(seq_len, xm_ref, xp_ref, xn_ref, win_ref, wgrp_ref, scale_ref, wout_ref,
                  gain_ref, bias_ref, conv_in_ref, o_ref, conv_out_ref, xe_ref, u_ref, lvl_ref):
    conv_out_ref[...] = conv_in_ref[...].astype(jnp.bfloat16)
    ts = xm_ref.shape[0]
    halo = POOL_HALO
    s = pl.program_id(1)
    n_s = pl.num_programs(1)
    xe_ref[0:halo, :] = xp_ref[...]
    xe_ref[halo:halo + ts, :] = xm_ref[...]
    xe_ref[halo + ts:, :] = xn_ref[...]
    tb = POOL_SUB_TILE
    n_ext = tb + 2 * halo
    n_sub = ts // tb
    for k in range(n_sub):
        r0 = k * tb
        u_ext = _dot(xe_ref[r0:r0 + n_ext, :].astype(jnp.bfloat16), win_ref[...])
        row = lax.broadcasted_iota(jnp.int32, (n_ext, 1), 0)
        if k == 0:
            u_ext = jnp.where((row < halo) & (s == 0), 0.0, u_ext)
        if k == n_sub - 1:
            u_ext = jnp.where((row >= halo + tb) & (s == n_s - 1), 0.0, u_ext)
        u_ref[k, 0:n_ext, :] = u_ext
        u_ref[k, n_ext:, :] = jnp.zeros((V7X_SUBLANES, D_MODEL), jnp.float32)

        t_abs = s * ts + r0 + lax.broadcasted_iota(jnp.int32, (tb, 1), 0)
        ys = []
        lvl = 0
        for g, w in enumerate(POOL_WINDOWS):
            half = w // 2
            cols = slice(g * POOL_GROUP_DIM, (g + 1) * POOL_GROUP_DIM)
            src = functools.partial(lambda k_, c_, lo, n: u_ref[k_, lo:lo + n, c_], k, cols)
            width = 1
            while 2 * width < w:
                lvl_ref[k, lvl, 0:n_ext, :] = src(0, n_ext) + src(width, n_ext)
                lvl_ref[k, lvl, n_ext:, :] = jnp.zeros((V7X_SUBLANES, POOL_GROUP_DIM), jnp.float32)
                src = functools.partial(lambda k_, l_, lo, n: lvl_ref[k_, l_, lo:lo + n, :], k, lvl)
                width, lvl = 2 * width, lvl + 1
            start = halo - half
            win_sum = src(start, tb) + src(start + width, tb)
            count = jnp.minimum(t_abs + half, seq_len) - jnp.maximum(t_abs - half, 0)
            mean = win_sum / count.astype(jnp.float32)
            mixed = (mean - u_ref[k, halo:halo + tb, cols]).astype(jnp.bfloat16)
            ys.append(_dot(mixed, wgrp_ref[g]))
        y = (jnp.concatenate(ys, axis=1) * scale_ref[...]).astype(jnp.bfloat16)
        m = _dot(y, wout_ref[...])
        o_ref[r0:r0 + tb, :] = _layer_norm(ALPHA * xm_ref[r0:r0 + tb, :] + m,
                                           gain_ref[...], bias_ref[...])


def _pool_ln(x, w_in, w_group, scale, w_out, gain, bias, convert):
    b, s_len, _ = x.shape
    ts = POOL_ROW_TILE
    halo = POOL_HALO
    blocks_per_tile = ts // halo
    n_halo_blocks = s_len // halo
    n_levels = sum(int(np.log2(w)) - 1 for w in POOL_WINDOWS)
    tb = POOL_SUB_TILE
    buf_rows = tb + 2 * halo + V7X_SUBLANES
    main_spec = pl.BlockSpec((None, ts, D_MODEL), lambda bi, si: (bi, si, 0))
    prev_spec = pl.BlockSpec(
        (None, halo, D_MODEL),
        lambda bi, si: (bi, jnp.maximum(si * blocks_per_tile - 1, 0), 0))
    next_spec = pl.BlockSpec(
        (None, halo, D_MODEL),
        lambda bi, si: (bi, jnp.minimum((si + 1) * blocks_per_tile, n_halo_blocks - 1), 0))
    n_s = s_len // ts
    conv_in, conv_out, conv_shapes = _convert_specs([convert], b * n_s, lambda bi, si: bi * n_s + si)
    outs = pl.pallas_call(
        functools.partial(_pool_ln_body, s_len),
        grid=(b, n_s),
        in_specs=[main_spec, prev_spec, next_spec,
                  _const_spec((D_MODEL, D_MODEL)),
                  _const_spec((N_POOL_GROUPS, POOL_GROUP_DIM, POOL_GROUP_DIM)),
                  _const_spec((1, D_MODEL)),
                  _const_spec((D_MODEL, D_MODEL)),
                  _const_spec((1, D_MODEL)), _const_spec((1, D_MODEL))] + conv_in,
        out_specs=[main_spec] + conv_out,
        out_shape=[jax.ShapeDtypeStruct(x.shape, jnp.float32)] + conv_shapes,
        scratch_shapes=[pltpu.VMEM((ts + 2 * halo, D_MODEL), jnp.float32),
                        pltpu.VMEM((ts // tb, buf_rows, D_MODEL), jnp.float32),
                        pltpu.VMEM((ts // tb, n_levels, buf_rows, POOL_GROUP_DIM), jnp.float32)],
        compiler_params=_params(2),
        name="pool_ln",
    )(x, x, x, w_in, w_group, scale, w_out, gain, bias, convert[0])
    return outs[0], outs[1:]


def _qkv_body(x_ref, w_ref, o0_ref, o1_ref, o2_ref, x_slabs):
    tm = x_ref.shape[0]
    n_slabs = D_MODEL // V7X_LANES
    for c in range(n_slabs):
        x_slabs[c] = x_ref[:, c * V7X_LANES:(c + 1) * V7X_LANES]
    for g, ((_, d), o_ref) in enumerate(zip(DIL_CONFIGS, (o0_ref, o1_ref, o2_ref))):
        n = tm // d
        if d == 1:
            xp = x_ref[...]
        else:
            xp = jnp.concatenate(
                [jnp.concatenate([x_slabs[c, pl.ds(r, n, stride=d), :] for r in range(d)], axis=0)
                 for c in range(n_slabs)], axis=1)
        xp = xp.astype(jnp.bfloat16)
        for j in range(3):
            lo = (g * 3 + j) * ATTN_WIDTH
            res = _dot(xp, w_ref[:, lo:lo + ATTN_WIDTH])
            if j == 0:
                res = res * (HEAD_DIM ** -0.5 * LOG2_E)
            res = res.astype(jnp.bfloat16)
            for r in range(d):
                o_ref[r, :, j * ATTN_WIDTH:(j + 1) * ATTN_WIDTH] = res[r * n:(r + 1) * n]


def _qkv_proj(x, w_qkv, batch, seq_len):
    tm = QKV_ROW_TILE
    tiles_per_seq = seq_len // tm
    gw = 3 * ATTN_WIDTH
    return pl.pallas_call(
        _qkv_body,
        grid=(batch * tiles_per_seq,),
        in_specs=[pl.BlockSpec((tm, D_MODEL), lambda i: (i, 0)),
                  _const_spec((D_MODEL, QKV_WIDTH))],
        out_specs=[pl.BlockSpec((None, d, tm // d, gw),
                                lambda i: (i // tiles_per_seq, 0, i % tiles_per_seq, 0))
                   for _, d in DIL_CONFIGS],
        out_shape=[jax.ShapeDtypeStruct((batch, d, seq_len // d, gw), jnp.bfloat16)
                   for _, d in DIL_CONFIGS],
        scratch_shapes=[pltpu.VMEM((D_MODEL // V7X_LANES, tm, V7X_LANES), jnp.float32)],
        compiler_params=_params(1),
        name="qkv_proj",
    )(x, w_qkv)


def _attn_bias(group):
    window, d = DIL_CONFIGS[group]
    qb, kb = ATTN_Q_BLOCK, ATTN_K_BLOCK
    r = (kb - qb) // 2
    a = np.arange(qb)[:, None]
    c = np.arange(kb)[None, :]
    rel = c - r - a
    band = np.abs(rel) <= r
    valid = np.stack([band, band & (c >= r), band & (c < qb + r)])
    neg_dist = np.where(valid, -(d * np.abs(rel)).astype(np.float64), MASK_VALUE).astype(np.float32)
    slopes = (_alibi_slopes()[group] * LOG2_E).astype(np.float32)
    return jnp.asarray(slopes)[None, :, None, None] * jnp.asarray(neg_dist)[:, None]


def _attn_body(n_blocks_total, q_ref, k_ref, kp_ref, kn_ref, v_ref, vp_ref, vn_ref, bias_ref,
               o_ref, stat_ref, ke_ref, ve_ref, s_ref, p_ref):
    n_res, tq, _ = q_ref.shape
    qb, kb = ATTN_Q_BLOCK, ATTN_K_BLOCK
    r = (kb - qb) // 2
    hq = qb // 2
    n_blk = tq // qb
    ti = pl.program_id(2)
    ke_ref[:, 0:r, :] = kp_ref[...]
    ke_ref[:, r:r + tq, :] = k_ref[...]
    ke_ref[:, r + tq:, :] = kn_ref[...]
    ve_ref[:, 0:r, :] = vp_ref[...]
    ve_ref[:, r:r + tq, :] = v_ref[...]
    ve_ref[:, r + tq:, :] = vn_ref[...]

    low_half = lax.broadcasted_iota(jnp.int32, (qb, V7X_LANES), 1) < HEAD_DIM
    lane = lax.broadcasted_iota(jnp.int32, (hq, STAT_WIDTH), 1)

    for res, blk in [(a, b) for a in range(n_res) for b in range(n_blk)]:
        row0 = blk * qb
        g_blk = ti * n_blk + blk
        variant = jnp.where(g_blk == 0, 1, 0) + jnp.where(g_blk == n_blocks_total - 1, 2, 0)
        stats = [jnp.zeros((hq, STAT_WIDTH), jnp.float32) for _ in range(2)]
        for pair in range(N_HEADS // 2):
            cols = slice(pair * V7X_LANES, (pair + 1) * V7X_LANES)
            buf = pair % 2
            q2 = q_ref[res, pl.ds(row0, qb), cols]
            zero = jnp.zeros_like(q2)
            q_stack = jnp.concatenate(
                [jnp.where(low_half, q2, zero), jnp.where(low_half, zero, q2)], axis=0)
            s_ref[buf] = lax.dot_general(q_stack, ke_ref[res, pl.ds(row0, kb), cols],
                                         (((1,), (1,)), ((), ())),
                                         preferred_element_type=jnp.float32)
            for c in range(4):
                head, half = 2 * pair + c // 2, c % 2
                rows = slice(c * hq, (c + 1) * hq)
                sc = s_ref[buf, rows, :] + bias_ref[variant, head, half * hq:(half + 1) * hq, :]
                m = jnp.max(sc, axis=1, keepdims=True)
                p = jnp.exp2(sc - m)
                l = jnp.sum(p, axis=1, keepdims=True)
                p_ref[buf, rows, :] = p.astype(jnp.bfloat16)
                stats[half] = jnp.where(lane == head, m, stats[half])
                stats[half] = jnp.where(lane == N_HEADS + head, l, stats[half])
            o2 = _dot(p_ref[buf], ve_ref[res, pl.ds(row0, kb), cols])
            o_ref[res, pl.ds(row0, qb), cols] = jnp.where(
                low_half, o2[:qb], o2[qb:]).astype(o_ref.dtype)
        for half in range(2):
            stat_ref[res, pl.ds(row0 + half * hq, hq), :] = stats[half]


def _attn_group(qkv_g, group):
    window, d = DIL_CONFIGS[group]
    batch, _, sub_len, _ = qkv_g.shape
    qb, kb = ATTN_Q_BLOCK, ATTN_K_BLOCK
    r = window // (2 * d)
    assert r == (kb - qb) // 2 and sub_len >= 2 * qb
    tq = min(ATTN_Q_TILE, sub_len)
    n_tiles = sub_len // tq
    n_res = min(d, ATTN_Q_TILE // tq)
    halo_per_tile = tq // r
    n_halo_blocks = sub_len // r

    def main_spec(j, width=ATTN_WIDTH):
        return pl.BlockSpec((None, n_res, tq, width), lambda bi, ri, ti: (bi, ri, ti, j))

    def prev_spec(j):
        return pl.BlockSpec(
            (None, n_res, r, ATTN_WIDTH),
            lambda bi, ri, ti: (bi, ri, jnp.maximum(ti * halo_per_tile - 1, 0), j))

    def next_spec(j):
        return pl.BlockSpec(
            (None, n_res, r, ATTN_WIDTH),
            lambda bi, ri, ti: (bi, ri, jnp.minimum((ti + 1) * halo_per_tile, n_halo_blocks - 1), j))

    return pl.pallas_call(
        functools.partial(_attn_body, sub_len // qb),
        grid=(batch, d // n_res, n_tiles),
        in_specs=[main_spec(0),
                  main_spec(1), prev_spec(1), next_spec(1),
                  main_spec(2), prev_spec(2), next_spec(2),
                  _const_spec((3, N_HEADS, qb, kb))],
        out_specs=[main_spec(0), main_spec(0, STAT_WIDTH)],
        out_shape=[jax.ShapeDtypeStruct((batch, d, sub_len, ATTN_WIDTH), jnp.bfloat16),
                   jax.ShapeDtypeStruct((batch, d, sub_len, STAT_WIDTH), jnp.float32)],
        scratch_shapes=[pltpu.VMEM((n_res, tq + 2 * r, ATTN_WIDTH), jnp.bfloat16),
                        pltpu.VMEM((n_res, tq + 2 * r, ATTN_WIDTH), jnp.bfloat16),
                        pltpu.VMEM((2, 2 * qb, kb), jnp.float32),
                        pltpu.VMEM((2, 2 * qb, kb), jnp.bfloat16)],
        compiler_params=_params(3),
        name=f"attn_group{group}",
    )(*([qkv_g] * 7), _attn_bias(group))


def _head_expand_matrix():
    k = np.arange(2 * STAT_WIDTH)[:, None] % STAT_WIDTH
    c = np.arange(ATTN_WIDTH)[None, :] // HEAD_DIM
    return jnp.asarray(k == c, dtype=jnp.bfloat16)


def _attn_out_ln_body(x_ref, o0_ref, o1_ref, o2_ref, l0_ref, l1_ref, l2_ref, e_ref, w_ref,
                      gain_ref, bias_ref, out_ref, o_scr, l_scr):
    o_refs = (o0_ref, o1_ref, o2_ref)
    l_refs = (l0_ref, l1_ref, l2_ref)

    def token_order(ref, scr, g, r0, ts):
        d, _, width = ref.shape
        if d == 1:
            return ref[0, r0:r0 + ts, :].astype(jnp.float32)
        n_slabs = width // V7X_LANES
        for r in range(d):
            piece = ref[r, r0 // d:(r0 + ts) // d, :].astype(jnp.float32)
            for c in range(n_slabs):
                scr[g, c, pl.ds(r0 + r, ts // d, stride=d), :] = (
                    piece[:, c * V7X_LANES:(c + 1) * V7X_LANES])
        return jnp.concatenate([scr[g, c, r0:r0 + ts, :] for c in range(n_slabs)], axis=1)

    ts = OUT_SUB_TILE
    is_head = lax.broadcasted_iota(jnp.int32, (ts, STAT_WIDTH), 1) < N_HEADS
    for r0 in range(0, x_ref.shape[0], ts):
        ms = [token_order(l_refs[g], l_scr, g, r0, ts) for g in range(N_DIL_GROUPS)]
        ls = [pltpu.roll(m, STAT_WIDTH - N_HEADS, 1) for m in ms]
        top = jnp.maximum(jnp.maximum(ms[0], ms[1]), ms[2])
        es = [jnp.exp2(m - top) for m in ms]
        inv = 1.0 / (es[0] * ls[0] + es[1] * ls[1] + es[2] * ls[2])
        acc = None
        for g in range(N_DIL_GROUPS):
            wt = jnp.where(is_head, es[g] * inv, 0.0)
            hi = wt.astype(jnp.bfloat16)
            lo = (wt - hi.astype(jnp.float32)).astype(jnp.bfloat16)
            w_full = _dot(jnp.concatenate([hi, lo], axis=1), e_ref[...])
            term = w_full * token_order(o_refs[g], o_scr, g, r0, ts)
            acc = term if acc is None else acc + term
        m = _dot(acc.astype(jnp.bfloat16), w_ref[...])
        out_ref[r0:r0 + ts, :] = _layer_norm(ALPHA * x_ref[r0:r0 + ts, :] + m,
                                             gain_ref[...], bias_ref[...])


def _attn_out_ln(x, os_, stats, w_out, gain, bias, seq_len):
    m = x.shape[0]
    tm = OUT_ROW_TILE
    tiles_per_seq = seq_len // tm
    row_spec = pl.BlockSpec((tm, D_MODEL), lambda i: (i, 0))

    def group_spec(d, width):
        return pl.BlockSpec((None, d, tm // d, width),
                            lambda i: (i // tiles_per_seq, 0, i % tiles_per_seq, 0))

    return pl.pallas_call(
        _attn_out_ln_body,
        grid=(m // tm,),
        in_specs=[row_spec]
                 + [group_spec(d, ATTN_WIDTH) for _, d in DIL_CONFIGS]
                 + [group_spec(d, STAT_WIDTH) for _, d in DIL_CONFIGS]
                 + [_const_spec((2 * STAT_WIDTH, ATTN_WIDTH)),
                    _const_spec((ATTN_WIDTH, D_MODEL)),
                    _const_spec((1, D_MODEL)), _const_spec((1, D_MODEL))],
        out_specs=row_spec,
        out_shape=jax.ShapeDtypeStruct((m, D_MODEL), jnp.float32),
        scratch_shapes=[
            pltpu.VMEM((N_DIL_GROUPS, ATTN_WIDTH // V7X_LANES, tm, V7X_LANES), jnp.float32),
            pltpu.VMEM((N_DIL_GROUPS, STAT_WIDTH // V7X_LANES, tm, V7X_LANES), jnp.float32)],
        compiler_params=_params(1),
        name="attn_out_ln",
    )(x, *os_, *stats, _head_expand_matrix(), w_out, gain, bias)


def kernel(x, ffn1_w_gate, ffn1_w_up, ffn1_w_down, ffn2_w_gate, ffn2_w_up, ffn2_w_down,
           ln_gain, ln_bias, pool_w_in, pool_w_group, pool_scale, pool_w_out,
           attn_w_qkv, attn_w_out):
    batch, seq_len, _ = x.shape
    bf16 = jnp.bfloat16
    n_mixers = 2
    h = x.reshape(batch * seq_len, D_MODEL)

    def ln_params(i, j):
        return ln_gain[i, j].reshape(1, D_MODEL), ln_bias[i, j].reshape(1, D_MODEL)

    ffn_stacks = [(ffn1_w_gate, ffn1_w_up, ffn1_w_down), (ffn2_w_gate, ffn2_w_up, ffn2_w_down)]
    ffn_order = [(which, i) for i in range(DEPTH) for which in range(2)]
    ffn_ready = [None]

    def ffn(k, h, gain, bias):
        which, layer = ffn_order[k]
        convert = ()
        if k + 1 < len(ffn_order):
            nxt_which, nxt_layer = ffn_order[k + 1]
            convert = [(w, nxt_layer) for w in ffn_stacks[nxt_which]]
        if ffn_ready[0] is None:
            h, ffn_ready[0] = _ffn_ln(h, *ffn_stacks[which], gain, bias, layer=layer, convert=convert)
        else:
            h, ffn_ready[0] = _ffn_ln(h, *ffn_ready[0], gain, bias, convert=convert)
        return h

    w_qkv_ready = {}
    for i in range(DEPTH):
        h = ffn(2 * i, h, *ln_params(i, 0))
        if i % n_mixers == 0:
            p = i // n_mixers
            assert i + 1 < DEPTH
            h, (w_qkv_ready[(i + 1) // n_mixers],) = _pool_ln(
                h.reshape(batch, seq_len, D_MODEL), pool_w_in[p].astype(bf16),
                pool_w_group[p].astype(bf16), pool_scale[p].reshape(1, D_MODEL),
                pool_w_out[p].astype(bf16), *ln_params(i, 1), (attn_w_qkv, (i + 1) // n_mixers))
            h = h.reshape(batch * seq_len, D_MODEL)
        else:
            a = i // n_mixers
            qkv = _qkv_proj(h, w_qkv_ready[a], batch, seq_len)
            outs = [_attn_group(qkv[g], g) for g in range(N_DIL_GROUPS)]
            h = _attn_out_ln(h, [o for o, _ in outs], [l for _, l in outs],
                             attn_w_out[a].astype(bf16), *ln_params(i, 1), seq_len)
        h = ffn(2 * i + 1, h, *ln_params(i, 2))
    return h.reshape(batch, seq_len, D_MODEL)
```

```python
import functools

import numpy as np
import jax
import jax.numpy as jnp
from jax import lax
from jax.experimental import pallas as pl
from jax.experimental.pallas import tpu as pltpu

D_MODEL = 1024
DEPTH = 2
POOL_WINDOWS = (2, 4, 8, 16)
N_POOL_GROUPS = len(POOL_WINDOWS)
POOL_GROUP_DIM = D_MODEL // N_POOL_GROUPS
POOL_HALO = max(POOL_WINDOWS) // 2
HEAD_DIM = 64
N_HEADS = D_MODEL // HEAD_DIM
DIL_CONFIGS = ((128, 1), (512, 4), (2048, 16))
N_DIL_GROUPS = len(DIL_CONFIGS)
ATTN_WIDTH = N_HEADS * HEAD_DIM
QKV_WIDTH = N_DIL_GROUPS * 3 * ATTN_WIDTH
D_FF = 2816
MACARON_WEIGHT = 0.5
ALPHA = (2.0 * DEPTH) ** 0.25
LN_EPS = 1e-5
MASK_VALUE = -1e30
LOG2_E = float(np.log2(np.e))

V7X_LANES = 128
V7X_MXU_DIM = 256
V7X_SUBLANES = 8
V7X_BF16_SUBLANES = 16
V7X_VMEM_LIMIT_BYTES = 56 * 1024 * 1024

FFN_ROW_TILE = 1024
FFN_SUB_TILE = 256
FFN_CHUNKS = ((0, 6 * V7X_MXU_DIM), (6 * V7X_MXU_DIM, D_FF))
WEIGHT_LOAD_CHUNKS = 8
POOL_ROW_TILE = 1024
POOL_SUB_TILE = 256
QKV_ROW_TILE = 512
ATTN_Q_TILE = 1024
ATTN_Q_BLOCK = 128
ATTN_K_BLOCK = 256
ATTN_SCORE_BUFFERS = 2
OUT_ROW_TILE = 1024
OUT_SUB_TILE = 256
STAT_WIDTH = V7X_LANES


def _alibi_slopes():
    n = N_DIL_GROUPS * N_HEADS
    s = 2.0 ** (-8.0 * np.arange(1, n + 1) / n)
    return s.reshape(N_DIL_GROUPS, N_HEADS)


def _layer_norm(y, gain, bias):
    mu = jnp.mean(y, axis=-1, keepdims=True)
    yc = y - mu
    var = jnp.mean(yc * yc, axis=-1, keepdims=True)
    return yc * lax.rsqrt(var + LN_EPS) * gain + bias


def _dot(a, b):
    return jnp.dot(a, b, preferred_element_type=jnp.float32)


def _const_spec(shape):
    return pl.BlockSpec(shape, lambda *_: (0,) * len(shape), pipeline_mode=pl.Buffered(1))


def _params(n_grid_dims):
    return pltpu.CompilerParams(
        dimension_semantics=("arbitrary",) * n_grid_dims,
        vmem_limit_bytes=V7X_VMEM_LIMIT_BYTES)


def _load_weight_bf16(src_hbm, dst_ref, stage_ref, sem):
    n_rows = stage_ref.shape[1]
    n_chunks = src_hbm.shape[0] // n_rows
    assert n_chunks * n_rows == src_hbm.shape[0]

    def chunk_copy(c):
        return pltpu.make_async_copy(src_hbm.at[pl.ds(c * n_rows, n_rows), :],
                                     stage_ref.at[c % 2], sem.at[c % 2])

    chunk_copy(0).start()
    for c in range(n_chunks):
        if c + 1 < n_chunks:
            chunk_copy(c + 1).start()
        chunk_copy(c).wait()
        dst_ref[c * n_rows:(c + 1) * n_rows, :] = stage_ref[c % 2].astype(jnp.bfloat16)


def _convert_specs(convert, n_steps, step_index):
    in_specs, out_specs, out_shapes = [], [], []
    for stack, layer in convert:
        _, rows, cols = stack.shape
        blk = rows // n_steps
        assert blk * n_steps == rows and blk % V7X_BF16_SUBLANES == 0
        in_specs.append(pl.BlockSpec(
            (None, blk, cols), lambda *ids, layer=layer: (layer, step_index(*ids), 0)))
        out_specs.append(pl.BlockSpec((blk, cols), lambda *ids: (step_index(*ids), 0)))
        out_shapes.append(jax.ShapeDtypeStruct((rows, cols), jnp.bfloat16))
    return in_specs, out_specs, out_shapes


def _ffn_ln_body(layer, n_conv, *refs):
    x_ref, wg_in, wu_in, wd_in, gain_ref, bias_ref = refs[:6]
    conv_in = refs[6:6 + n_conv]
    o_ref = refs[6 + n_conv]
    conv_out = refs[7 + n_conv:7 + 2 * n_conv]
    scratch = refs[7 + 2 * n_conv:]
    if layer is None:
        wg_ref, wu_ref, wd_ref = wg_in, wu_in, wd_in
    else:
        wg_ref, wu_ref, wd_ref, stage_in_ref, stage_down_ref, sem = scratch

        @pl.when(pl.program_id(0) == 0)
        def _():
            _load_weight_bf16(wg_in.at[layer], wg_ref, stage_in_ref, sem)
            _load_weight_bf16(wu_in.at[layer], wu_ref, stage_in_ref, sem)
            _load_weight_bf16(wd_in.at[layer], wd_ref, stage_down_ref, sem)

    for src, dst in zip(conv_in, conv_out):
        dst[...] = src[...].astype(jnp.bfloat16)

    for r0 in range(0, x_ref.shape[0], FFN_SUB_TILE):
        rows = slice(r0, r0 + FFN_SUB_TILE)
        x = x_ref[rows, :]
        xb = x.astype(jnp.bfloat16)
        h = None
        for lo, hi in FFN_CHUNKS:
            g = _dot(xb, wg_ref[:, lo:hi])
            u = _dot(xb, wu_ref[:, lo:hi])
            hg = 0.5 * g
            a = ((hg + hg * jnp.tanh(hg)) * u).astype(jnp.bfloat16)
            part = _dot(a, wd_ref[lo:hi, :])
            h = part if h is None else h + part
        y = ALPHA * x + MACARON_WEIGHT * h
        o_ref[rows, :] = _layer_norm(y, gain_ref[...], bias_ref[...])


def _ffn_ln(x, w_gate, w_up, w_down, gain, bias, layer=None, convert=()):
    m = x.shape[0]
    tm = FFN_ROW_TILE
    n_steps = m // tm
    row_spec = pl.BlockSpec((tm, D_MODEL), lambda i: (i, 0))
    if layer is None:
        weight_specs = [_const_spec((D_MODEL, D_FF)), _const_spec((D_MODEL, D_FF)),
                        _const_spec((D_FF, D_MODEL))]
        scratch = []
    else:
        weight_specs = [pl.BlockSpec(memory_space=pl.ANY)] * 3
        scratch = [pltpu.VMEM((D_MODEL, D_FF), jnp.bfloat16),
                   pltpu.VMEM((D_MODEL, D_FF), jnp.bfloat16),
                   pltpu.VMEM((D_FF, D_MODEL), jnp.bfloat16),
                   pltpu.VMEM((2, D_MODEL // WEIGHT_LOAD_CHUNKS, D_FF), jnp.float32),
                   pltpu.VMEM((2, D_FF // WEIGHT_LOAD_CHUNKS, D_MODEL), jnp.float32),
                   pltpu.SemaphoreType.DMA((2,))]
    conv_in, conv_out, conv_shapes = _convert_specs(convert, n_steps, lambda i: i)
    outs = pl.pallas_call(
        functools.partial(_ffn_ln_body, layer, len(convert)),
        grid=(n_steps,),
        in_specs=[row_spec] + weight_specs
                 + [_const_spec((1, D_MODEL)), _const_spec((1, D_MODEL))] + conv_in,
        out_specs=[row_spec] + conv_out,
        out_shape=[jax.ShapeDtypeStruct((m, D_MODEL), jnp.float32)] + conv_shapes,
        scratch_shapes=scratch,
        compiler_params=_params(1),
        name="ffn_ln",
    )(x, w_gate, w_up, w_down, gain, bias, *[stack for stack, _ in convert])
    return outs[0], outs[1:]


def _pool_ln_body(seq_len, xm_ref, xp_ref, xn_ref, win_ref, wgrp_ref, scale_ref, wout_ref,
                  gain_ref, bias_ref, conv_in_ref, o_ref, conv_out_ref, xe_ref, u_ref, lvl_ref):
    conv_out_ref[...] = conv_in_ref[...].astype(jnp.bfloat16)
    ts = xm_ref.shape[0]
    halo = POOL_HALO
    s = pl.program_id(1)
    n_s = pl.num_programs(1)
    xe_ref[0:halo, :] = xp_ref[...]
    xe_ref[halo:halo + ts, :] = xm_ref[...]
    xe_ref[halo + ts:, :] = xn_ref[...]
    tb = POOL_SUB_TILE
    n_ext = tb + 2 * halo
    n_sub = ts // tb
    for k in range(n_sub):
        r0 = k * tb
        u_ext = _dot(xe_ref[r0:r0 + n_ext, :].astype(jnp.bfloat16), win_ref[...])
        row = lax.broadcasted_iota(jnp.int32, (n_ext, 1), 0)
        if k == 0:
            u_ext = jnp.where((row < halo) & (s == 0), 0.0, u_ext)
        if k == n_sub - 1:
            u_ext = jnp.where((row >= halo + tb) & (s == n_s - 1), 0.0, u_ext)
        u_ref[k, 0:n_ext, :] = u_ext
        u_ref[k, n_ext:, :] = jnp.zeros((V7X_SUBLANES, D_MODEL), jnp.float32)

        t_abs = s * ts + r0 + lax.broadcasted_iota(jnp.int32, (tb, 1), 0)
        ys = []
        lvl = 0
        for g, w in enumerate(POOL_WINDOWS):
            half = w // 2
            cols = slice(g * POOL_GROUP_DIM, (g + 1) * POOL_GROUP_DIM)
            src = functools.partial(lambda k_, c_, lo, n: u_ref[k_, lo:lo + n, c_], k, cols)
            width = 1
            while 2 * width < w:
                lvl_ref[k, lvl, 0:n_ext, :] = src(0, n_ext) + src(width, n_ext)
                lvl_ref[k, lvl, n_ext:, :] = jnp.zeros((V7X_SUBLANES, POOL_GROUP_DIM), jnp.float32)
                src = functools.partial(lambda k_, l_, lo, n: lvl_ref[k_, l_, lo:lo + n, :], k, lvl)
                width, lvl = 2 * width, lvl + 1
            start = halo - half
            win_sum = src(start, tb) + src(start + width, tb)
            count = jnp.minimum(t_abs + half, seq_len) - jnp.maximum(t_abs - half, 0)
            mean = win_sum / count.astype(jnp.float32)
            mixed = (mean - u_ref[k, halo:halo + tb, cols]).astype(jnp.bfloat16)
            ys.append(_dot(mixed, wgrp_ref[g]))
        y = (jnp.concatenate(ys, axis=1) * scale_ref[...]).astype(jnp.bfloat16)
        m = _dot(y, wout_ref[...])
        o_ref[r0:r0 + tb, :] = _layer_norm(ALPHA * xm_ref[r0:r0 + tb, :] + m,
                                           gain_ref[...], bias_ref[...])


def _pool_ln(x, w_in, w_group, scale, w_out, gain, bias, convert):
    b, s_len, _ = x.shape
    ts = POOL_ROW_TILE
    halo = POOL_HALO
    blocks_per_tile = ts // halo
    n_halo_blocks = s_len // halo
    n_levels = sum(int(np.log2(w)) - 1 for w in POOL_WINDOWS)
    tb = POOL_SUB_TILE
    buf_rows = tb + 2 * halo + V7X_SUBLANES
    main_spec = pl.BlockSpec((None, ts, D_MODEL), lambda bi, si: (bi, si, 0))
    prev_spec = pl.BlockSpec(
        (None, halo, D_MODEL),
        lambda bi, si: (bi, jnp.maximum(si * blocks_per_tile - 1, 0), 0))
    next_spec = pl.BlockSpec(
        (None, halo, D_MODEL),
        lambda bi, si: (bi, jnp.minimum((si + 1) * blocks_per_tile, n_halo_blocks - 1), 0))
    n_s = s_len // ts
    conv_in, conv_out, conv_shapes = _convert_specs([convert], b * n_s, lambda bi, si: bi * n_s + si)
    outs = pl.pallas_call(
        functools.partial(_pool_ln_body, s_len),
        grid=(b, n_s),
        in_specs=[main_spec, prev_spec, next_spec,
                  _const_spec((D_MODEL, D_MODEL)),
                  _const_spec((N_POOL_GROUPS, POOL_GROUP_DIM, POOL_GROUP_DIM)),
                  _const_spec((1, D_MODEL)),
                  _const_spec((D_MODEL, D_MODEL)),
                  _const_spec((1, D_MODEL)), _const_spec((1, D_MODEL))] + conv_in,
        out_specs=[main_spec] + conv_out,
        out_shape=[jax.ShapeDtypeStruct(x.shape, jnp.float32)] + conv_shapes,
        scratch_shapes=[pltpu.VMEM((ts + 2 * halo, D_MODEL), jnp.float32),
                        pltpu.VMEM((ts // tb, buf_rows, D_MODEL), jnp.float32),
                        pltpu.VMEM((ts // tb, n_levels, buf_rows, POOL_GROUP_DIM), jnp.float32)],
        compiler_params=_params(2),
        name="pool_ln",
    )(x, x, x, w_in, w_group, scale, w_out, gain, bias, convert[0])
    return outs[0], outs[1:]


def _qkv_body(x_ref, w_ref, o0_ref, o1_ref, o2_ref, x_slabs):
    tm = x_ref.shape[0]
    n_slabs = D_MODEL // V7X_LANES
    for c in range(n_slabs):
        x_slabs[c] = x_ref[:, c * V7X_LANES:(c + 1) * V7X_LANES]
    for g, ((_, d), o_ref) in enumerate(zip(DIL_CONFIGS, (o0_ref, o1_ref, o2_ref))):
        n = tm // d
        if d == 1:
            xp = x_ref[...]
        else:
            xp = jnp.concatenate(
                [jnp.concatenate([x_slabs[c, pl.ds(r, n, stride=d), :] for r in range(d)], axis=0)
                 for c in range(n_slabs)], axis=1)
        xp = xp.astype(jnp.bfloat16)
        for j in range(3):
            lo = (g * 3 + j) * ATTN_WIDTH
            res = _dot(xp, w_ref[:, lo:lo + ATTN_WIDTH])
            if j == 0:
                res = res * (HEAD_DIM ** -0.5 * LOG2_E)
            res = res.astype(jnp.bfloat16)
            for r in range(d):
                o_ref[r, :, j * ATTN_WIDTH:(j + 1) * ATTN_WIDTH] = res[r * n:(r + 1) * n]


def _qkv_proj(x, w_qkv, batch, seq_len):
    tm = QKV_ROW_TILE
    tiles_per_seq = seq_len // tm
    gw = 3 * ATTN_WIDTH
    return pl.pallas_call(
        _qkv_body,
        grid=(batch * tiles_per_seq,),
        in_specs=[pl.BlockSpec((tm, D_MODEL), lambda i: (i, 0)),
                  _const_spec((D_MODEL, QKV_WIDTH))],
        out_specs=[pl.BlockSpec((None, d, tm // d, gw),
                                lambda i: (i // tiles_per_seq, 0, i % tiles_per_seq, 0))
                   for _, d in DIL_CONFIGS],
        out_shape=[jax.ShapeDtypeStruct((batch, d, seq_len // d, gw), jnp.bfloat16)
                   for _, d in DIL_CONFIGS],
        scratch_shapes=[pltpu.VMEM((D_MODEL // V7X_LANES, tm, V7X_LANES), jnp.float32)],
        compiler_params=_params(1),
        name="qkv_proj",
    )(x, w_qkv)


def _attn_bias(group):
    window, d = DIL_CONFIGS[group]
    qb, kb = ATTN_Q_BLOCK, ATTN_K_BLOCK
    r = (kb - qb) // 2
    a = np.arange(qb)[:, None]
    c = np.arange(kb)[None, :]
    rel = c - r - a
    band = np.abs(rel) <= r
    valid = np.stack([band, band & (c >= r), band & (c < qb + r)])
    neg_dist = np.where(valid, -(d * np.abs(rel)).astype(np.float64), MASK_VALUE).astype(np.float32)
    slopes = (_alibi_slopes()[group] * LOG2_E).astype(np.float32)
    return jnp.asarray(slopes)[None, :, None, None] * jnp.asarray(neg_dist)[:, None]


def _attn_body(n_blocks_total, q_ref, k_ref, kp_ref, kn_ref, v_ref, vp_ref, vn_ref, bias_ref,
               o_ref, stat_ref, ke_ref, ve_ref, s_ref, p_ref):
    n_res, tq, _ = q_ref.shape
    qb, kb = ATTN_Q_BLOCK, ATTN_K_BLOCK
    r = (kb - qb) // 2
    hq = qb // 2
    n_blk = tq // qb
    ti = pl.program_id(2)
    ke_ref[:, 0:r, :] = kp_ref[...]
    ke_ref[:, r:r + tq, :] = k_ref[...]
    ke_ref[:, r + tq:, :] = kn_ref[...]
    for pair in range(N_HEADS // 2):
        src = slice(pair * V7X_LANES, (pair + 1) * V7X_LANES)
        dst = slice(2 * pair * V7X_LANES, (2 * pair + 1) * V7X_LANES)
        one = slice((2 * pair + 1) * V7X_LANES, (2 * pair + 2) * V7X_LANES)
        ve_ref[:, 0:r, dst] = vp_ref[:, :, src]
        ve_ref[:, r:r + tq, dst] = v_ref[:, :, src]
        ve_ref[:, r + tq:, dst] = vn_ref[:, :, src]
        ve_ref[:, :, one] = jnp.ones((n_res, tq + 2 * r, V7X_LANES), jnp.bfloat16)

    low_half = lax.broadcasted_iota(jnp.int32, (qb, V7X_LANES), 1) < HEAD_DIM

    for res, blk in [(a, b) for a in range(n_res) for b in range(n_blk)]:
        row0 = blk * qb
        g_blk = ti * n_blk + blk
        variant = jnp.where(g_blk == 0, 1, 0) + jnp.where(g_blk == n_blocks_total - 1, 2, 0)
        stat_ref[res, pl.ds(row0, qb), :] = jnp.zeros((qb, STAT_WIDTH), jnp.float32)
        for pair in range(N_HEADS // 2):
            cols = slice(pair * V7X_LANES, (pair + 1) * V7X_LANES)
            buf = pair % ATTN_SCORE_BUFFERS
            q2 = q_ref[res, pl.ds(row0, qb), cols]
            zero = jnp.zeros_like(q2)
            q_stack = jnp.concatenate(
                [jnp.where(low_half, q2, zero), jnp.where(low_half, zero, q2)], axis=0)
            s_ref[buf] = lax.dot_general(q_stack, ke_ref[res, pl.ds(row0, kb), cols],
                                         (((1,), (1,)), ((), ())),
                                         preferred_element_type=jnp.float32)
            for c in range(4):
                head, half = 2 * pair + c // 2, c % 2
                rows = slice(c * hq, (c + 1) * hq)
                sc = s_ref[buf, rows, :] + bias_ref[variant, head, half * hq:(half + 1) * hq, :]
                m = jnp.max(sc, axis=1, keepdims=True)
                p_ref[buf, rows, :] = jnp.exp2(sc - m).astype(jnp.bfloat16)
                stat_ref[res, pl.ds(row0 + half * hq, hq), head:head + 1] = m
            v_cols = slice(2 * pair * V7X_LANES, (2 * pair + 2) * V7X_LANES)
            o2 = _dot(p_ref[buf], ve_ref[res, pl.ds(row0, kb), v_cols])
            o_ref[res, pl.ds(row0, qb), cols] = jnp.where(
                low_half, o2[:qb, :V7X_LANES], o2[qb:, :V7X_LANES]).astype(o_ref.dtype)
            for hh in range(2):
                lane_l = N_HEADS + 2 * pair + hh
                stat_ref[res, pl.ds(row0, qb), lane_l:lane_l + 1] = (
                    o2[hh * qb:(hh + 1) * qb, V7X_LANES + lane_l:V7X_LANES + lane_l + 1])


def _attn_group(qkv_g, group):
    window, d = DIL_CONFIGS[group]
    batch, _, sub_len, _ = qkv_g.shape
    qb, kb = ATTN_Q_BLOCK, ATTN_K_BLOCK
    r = window // (2 * d)
    assert r == (kb - qb) // 2 and sub_len >= 2 * qb
    tq = min(ATTN_Q_TILE, sub_len)
    n_tiles = sub_len // tq
    n_res = min(d, ATTN_Q_TILE // tq)
    halo_per_tile = tq // r
    n_halo_blocks = sub_len // r

    def main_spec(j, width=ATTN_WIDTH):
        return pl.BlockSpec((None, n_res, tq, width), lambda bi, ri, ti: (bi, ri, ti, j))

    def prev_spec(j):
        return pl.BlockSpec(
            (None, n_res, r, ATTN_WIDTH),
            lambda bi, ri, ti: (bi, ri, jnp.maximum(ti * halo_per_tile - 1, 0), j))

    def next_spec(j):
        return pl.BlockSpec(
            (None, n_res, r, ATTN_WIDTH),
            lambda bi, ri, ti: (bi, ri, jnp.minimum((ti + 1) * halo_per_tile, n_halo_blocks - 1), j))

    return pl.pallas_call(
        functools.partial(_attn_body, sub_len // qb),
        grid=(batch, d // n_res, n_tiles),
        in_specs=[main_spec(0),
                  main_spec(1), prev_spec(1), next_spec(1),
                  main_spec(2), prev_spec(2), next_spec(2),
                  _const_spec((3, N_HEADS, qb, kb))],
        out_specs=[main_spec(0), main_spec(0, STAT_WIDTH)],
        out_shape=[jax.ShapeDtypeStruct((batch, d, sub_len, ATTN_WIDTH), jnp.bfloat16),
                   jax.ShapeDtypeStruct((batch, d, sub_len, STAT_WIDTH), jnp.float32)],
        scratch_shapes=[pltpu.VMEM((n_res, tq + 2 * r, ATTN_WIDTH), jnp.bfloat16),
                        pltpu.VMEM((n_res, tq + 2 * r, 2 * ATTN_WIDTH), jnp.bfloat16),
                        pltpu.VMEM((ATTN_SCORE_BUFFERS, 2 * qb, kb), jnp.float32),
                        pltpu.VMEM((ATTN_SCORE_BUFFERS, 2 * qb, kb), jnp.bfloat16)],
        compiler_params=_params(3),
        name=f"attn_group{group}",
    )(*([qkv_g] * 7), _attn_bias(group))


def _head_expand_matrix():
    k = np.arange(2 * STAT_WIDTH)[:, None] % STAT_WIDTH
    c = np.arange(ATTN_WIDTH)[None, :] // HEAD_DIM
    return jnp.asarray(k == c, dtype=jnp.bfloat16)


def _attn_out_ln_body(x_ref, o0_ref, o1_ref, o2_ref, l0_ref, l1_ref, l2_ref, e_ref, w_ref,
                      gain_ref, bias_ref, out_ref, o_scr, l_scr):
    o_refs = (o0_ref, o1_ref, o2_ref)
    l_refs = (l0_ref, l1_ref, l2_ref)

    def token_order(ref, scr, g, r0, ts):
        assert DIL_CONFIGS[0][1] == 1 and all(dil > 1 for _, dil in DIL_CONFIGS[1:])
        d, _, width = ref.shape
        if d == 1:
            return ref[0, r0:r0 + ts, :].astype(jnp.float32)
        n_slabs = width // V7X_LANES
        for r in range(d):
            piece = ref[r, r0 // d:(r0 + ts) // d, :].astype(jnp.float32)
            for c in range(n_slabs):
                scr[g - 1, c, pl.ds(r0 + r, ts // d, stride=d), :] = (
                    piece[:, c * V7X_LANES:(c + 1) * V7X_LANES])
        return jnp.concatenate([scr[g - 1, c, r0:r0 + ts, :] for c in range(n_slabs)], axis=1)

    ts = OUT_SUB_TILE
    is_head = lax.broadcasted_iota(jnp.int32, (ts, STAT_WIDTH), 1) < N_HEADS
    for r0 in range(0, x_ref.shape[0], ts):
        ms = [token_order(l_refs[g], l_scr, g, r0, ts) for g in range(N_DIL_GROUPS)]
        ls = [pltpu.roll(m, STAT_WIDTH - N_HEADS, 1) for m in ms]
        top = jnp.maximum(jnp.maximum(ms[0], ms[1]), ms[2])
        es = [jnp.exp2(m - top) for m in ms]
        inv = 1.0 / (es[0] * ls[0] + es[1] * ls[1] + es[2] * ls[2])
        acc = None
        for g in range(N_DIL_GROUPS):
            wt = jnp.where(is_head, es[g] * inv, 0.0)
            hi = wt.astype(jnp.bfloat16)
            lo = (wt - hi.astype(jnp.float32)).astype(jnp.bfloat16)
            w_full = _dot(jnp.concatenate([hi, lo], axis=1), e_ref[...])
            term = w_full * token_order(o_refs[g], o_scr, g, r0, ts)
            acc = term if acc is None else acc + term
        m = _dot(acc.astype(jnp.bfloat16), w_ref[...])
        out_ref[r0:r0 + ts, :] = _layer_norm(ALPHA * x_ref[r0:r0 + ts, :] + m,
                                             gain_ref[...], bias_ref[...])


def _attn_out_ln(x, os_, stats, w_out, gain, bias, seq_len):
    m = x.shape[0]
    tm = OUT_ROW_TILE
    tiles_per_seq = seq_len // tm
    row_spec = pl.BlockSpec((tm, D_MODEL), lambda i: (i, 0))

    def group_spec(d, width):
        return pl.BlockSpec((None, d, tm // d, width),
                            lambda i: (i // tiles_per_seq, 0, i % tiles_per_seq, 0))

    return pl.pallas_call(
        _attn_out_ln_body,
        grid=(m // tm,),
        in_specs=[row_spec]
                 + [group_spec(d, ATTN_WIDTH) for _, d in DIL_CONFIGS]
                 + [group_spec(d, STAT_WIDTH) for _, d in DIL_CONFIGS]
                 + [_const_spec((2 * STAT_WIDTH, ATTN_WIDTH)),
                    _const_spec((ATTN_WIDTH, D_MODEL)),
                    _const_spec((1, D_MODEL)), _const_spec((1, D_MODEL))],
        out_specs=row_spec,
        out_shape=jax.ShapeDtypeStruct((m, D_MODEL), jnp.float32),
        scratch_shapes=[
            pltpu.VMEM((N_DIL_GROUPS - 1, ATTN_WIDTH // V7X_LANES, tm, V7X_LANES), jnp.float32),
            pltpu.VMEM((N_DIL_GROUPS - 1, STAT_WIDTH // V7X_LANES, tm, V7X_LANES), jnp.float32)],
        compiler_params=_params(1),
        name="attn_out_ln",
    )(x, *os_, *stats, _head_expand_matrix(), w_out, gain, bias)


def kernel(x, ffn1_w_gate, ffn1_w_up, ffn1_w_down, ffn2_w_gate, ffn2_w_up, ffn2_w_down,
           ln_gain, ln_bias, pool_w_in, pool_w_group, pool_scale, pool_w_out,
           attn_w_qkv, attn_w_out):
    batch, seq_len, _ = x.shape
    bf16 = jnp.bfloat16
    n_mixers = 2
    h = x.reshape(batch * seq_len, D_MODEL)

    def ln_params(i, j):
        return ln_gain[i, j].reshape(1, D_MODEL), ln_bias[i, j].reshape(1, D_MODEL)

    ffn_stacks = [(ffn1_w_gate, ffn1_w_up, ffn1_w_down), (ffn2_w_gate, ffn2_w_up, ffn2_w_down)]
    ffn_order = [(which, i) for i in range(DEPTH) for which in range(2)]
    ffn_ready = [None]

    def ffn(k, h, gain, bias):
        which, layer = ffn_order[k]
        convert = ()
        if k + 1 < len(ffn_order):
            nxt_which, nxt_layer = ffn_order[k + 1]
            convert = [(w, nxt_layer) for w in ffn_stacks[nxt_which]]
        if ffn_ready[0] is None:
            h, ffn_ready[0] = _ffn_ln(h, *ffn_stacks[which], gain, bias, layer=layer, convert=convert)
        else:
            h, ffn_ready[0] = _ffn_ln(h, *ffn_ready[0], gain, bias, convert=convert)
        return h

    w_qkv_ready = {}
    for i in range(DEPTH):
        h = ffn(2 * i, h, *ln_params(i, 0))
        if i % n_mixers == 0:
            p = i // n_mixers
            assert i + 1 < DEPTH
            h, (w_qkv_ready[(i + 1) // n_mixers],) = _pool_ln(
                h.reshape(batch, seq_len, D_MODEL), pool_w_in[p].astype(bf16),
                pool_w_group[p].astype(bf16), pool_scale[p].reshape(1, D_MODEL),
                pool_w_out[p].astype(bf16), *ln_params(i, 1), (attn_w_qkv, (i + 1) // n_mixers))
            h = h.reshape(batch * seq_len, D_MODEL)
        else:
            a = i // n_mixers
            qkv = _qkv_proj(h, w_qkv_ready[a], batch, seq_len)
            outs = [_attn_group(qkv[g], g) for g in range(N_DIL_GROUPS)]
            h = _attn_out_ln(h, [o for o, _ in outs], [l for _, l in outs],
                             attn_w_out[a].astype(bf16), *ln_params(i, 1), seq_len)
        h = ffn(2 * i + 1, h, *ln_params(i, 2))
    return h.reshape(batch, seq_len, D_MODEL)
```

```python
import functools

import numpy as np
import jax
import jax.numpy as jnp
from jax import lax
from jax.experimental import pallas as pl
from jax.experimental.pallas import tpu as pltpu

D_MODEL = 1024
DEPTH = 2
POOL_WINDOWS = (2, 4, 8, 16)
N_POOL_GROUPS = len(POOL_WINDOWS)
POOL_GROUP_DIM = D_MODEL // N_POOL_GROUPS
POOL_HALO = max(POOL_WINDOWS) // 2
HEAD_DIM = 64
N_HEADS = D_MODEL // HEAD_DIM
DIL_CONFIGS = ((128, 1), (512, 4), (2048, 16))
N_DIL_GROUPS = len(DIL_CONFIGS)
ATTN_WIDTH = N_HEADS * HEAD_DIM
QKV_WIDTH = N_DIL_GROUPS * 3 * ATTN_WIDTH
D_FF = 2816
MACARON_WEIGHT = 0.5
ALPHA = (2.0 * DEPTH) ** 0.25
LN_EPS = 1e-5
MASK_VALUE = -1e30
LOG2_E = float(np.log2(np.e))

V7X_LANES = 128
V7X_MXU_DIM = 256
V7X_SUBLANES = 8
V7X_BF16_SUBLANES = 16
V7X_VMEM_LIMIT_BYTES = 56 * 1024 * 1024

FFN_ROW_TILE = 1024
FFN_SUB_TILE = 256
FFN_CHUNKS = ((0, 6 * V7X_MXU_DIM), (6 * V7X_MXU_DIM, D_FF))
WEIGHT_LOAD_CHUNKS = 8
POOL_ROW_TILE = 1024
POOL_SUB_TILE = 256
QKV_ROW_TILE = 512
QKV_SUB_TILE = 256
ATTN_Q_TILE = 1024
ATTN_Q_BLOCK = 128
ATTN_K_BLOCK = 256
ATTN_SCORE_BUFFERS = 2
OUT_ROW_TILE = 1024
OUT_SUB_TILE = 256
STAT_WIDTH = V7X_LANES


def _alibi_slopes():
    n = N_DIL_GROUPS * N_HEADS
    s = 2.0 ** (-8.0 * np.arange(1, n + 1) / n)
    return s.reshape(N_DIL_GROUPS, N_HEADS)


def _layer_norm(y, gain, bias):
    mu = jnp.mean(y, axis=-1, keepdims=True)
    yc = y - mu
    var = jnp.mean(yc * yc, axis=-1, keepdims=True)
    return yc * lax.rsqrt(var + LN_EPS) * gain + bias


def _dot(a, b):
    return jnp.dot(a, b, preferred_element_type=jnp.float32)


def _const_spec(shape):
    return pl.BlockSpec(shape, lambda *_: (0,) * len(shape), pipeline_mode=pl.Buffered(1))


def _params(n_grid_dims):
    return pltpu.CompilerParams(
        dimension_semantics=("arbitrary",) * n_grid_dims,
        vmem_limit_bytes=V7X_VMEM_LIMIT_BYTES)


def _load_weight_bf16(src_hbm, dst_ref, stage_ref, sem):
    n_rows = stage_ref.shape[1]
    n_chunks = src_hbm.shape[0] // n_rows
    assert n_chunks * n_rows == src_hbm.shape[0]

    def chunk_copy(c):
        return pltpu.make_async_copy(src_hbm.at[pl.ds(c * n_rows, n_rows), :],
                                     stage_ref.at[c % 2], sem.at[c % 2])

    chunk_copy(0).start()
    for c in range(n_chunks):
        if c + 1 < n_chunks:
            chunk_copy(c + 1).start()
        chunk_copy(c).wait()
        dst_ref[c * n_rows:(c + 1) * n_rows, :] = stage_ref[c % 2].astype(jnp.bfloat16)


def _convert_specs(convert, n_steps, step_index):
    in_specs, out_specs, out_shapes = [], [], []
    for stack, layer in convert:
        _, rows, cols = stack.shape
        blk = rows // n_steps
        assert blk * n_steps == rows and blk % V7X_BF16_SUBLANES == 0
        in_specs.append(pl.BlockSpec(
            (None, blk, cols), lambda *ids, layer=layer: (layer, step_index(*ids), 0)))
        out_specs.append(pl.BlockSpec((blk, cols), lambda *ids: (step_index(*ids), 0)))
        out_shapes.append(jax.ShapeDtypeStruct((rows, cols), jnp.bfloat16))
    return in_specs, out_specs, out_shapes


def _ffn_ln_body(layer, n_conv, *refs):
    x_ref, wg_in, wu_in, wd_in, gain_ref, bias_ref = refs[:6]
    conv_in = refs[6:6 + n_conv]
    o_ref = refs[6 + n_conv]
    conv_out = refs[7 + n_conv:7 + 2 * n_conv]
    scratch = refs[7 + 2 * n_conv:]
    if layer is None:
        wg_ref, wu_ref, wd_ref = wg_in, wu_in, wd_in
    else:
        wg_ref, wu_ref, wd_ref, stage_in_ref, stage_down_ref, sem = scratch

        @pl.when(pl.program_id(0) == 0)
        def _():
            _load_weight_bf16(wg_in.at[layer], wg_ref, stage_in_ref, sem)
            _load_weight_bf16(wu_in.at[layer], wu_ref, stage_in_ref, sem)
            _load_weight_bf16(wd_in.at[layer], wd_ref, stage_down_ref, sem)

    for src, dst in zip(conv_in, conv_out):
        dst[...] = src[...].astype(jnp.bfloat16)

    for r0 in range(0, x_ref.shape[0], FFN_SUB_TILE):
        rows = slice(r0, r0 + FFN_SUB_TILE)
        x = x_ref[rows, :]
        xb = x.astype(jnp.bfloat16)
        h = None
        for lo, hi in FFN_CHUNKS:
            g = _dot(xb, wg_ref[:, lo:hi])
            u = _dot(xb, wu_ref[:, lo:hi])
            hg = 0.5 * g
            a = ((hg + hg * jnp.tanh(hg)) * u).astype(jnp.bfloat16)
            part = _dot(a, wd_ref[lo:hi, :])
            h = part if h is None else h + part
        y = ALPHA * x + MACARON_WEIGHT * h
        o_ref[rows, :] = _layer_norm(y, gain_ref[...], bias_ref[...])


def _ffn_ln(x, w_gate, w_up, w_down, gain, bias, layer=None, convert=()):
    m = x.shape[0]
    tm = FFN_ROW_TILE
    n_steps = m // tm
    row_spec = pl.BlockSpec((tm, D_MODEL), lambda i: (i, 0))
    if layer is None:
        weight_specs = [_const_spec((D_MODEL, D_FF)), _const_spec((D_MODEL, D_FF)),
                        _const_spec((D_FF, D_MODEL))]
        scratch = []
    else:
        weight_specs = [pl.BlockSpec(memory_space=pl.ANY)] * 3
        scratch = [pltpu.VMEM((D_MODEL, D_FF), jnp.bfloat16),
                   pltpu.VMEM((D_MODEL, D_FF), jnp.bfloat16),
                   pltpu.VMEM((D_FF, D_MODEL), jnp.bfloat16),
                   pltpu.VMEM((2, D_MODEL // WEIGHT_LOAD_CHUNKS, D_FF), jnp.float32),
                   pltpu.VMEM((2, D_FF // WEIGHT_LOAD_CHUNKS, D_MODEL), jnp.float32),
                   pltpu.SemaphoreType.DMA((2,))]
    conv_in, conv_out, conv_shapes = _convert_specs(convert, n_steps, lambda i: i)
    outs = pl.pallas_call(
        functools.partial(_ffn_ln_body, layer, len(convert)),
        grid=(n_steps,),
        in_specs=[row_spec] + weight_specs
                 + [_const_spec((1, D_MODEL)), _const_spec((1, D_MODEL))] + conv_in,
        out_specs=[row_spec] + conv_out,
        out_shape=[jax.ShapeDtypeStruct((m, D_MODEL), jnp.float32)] + conv_shapes,
        scratch_shapes=scratch,
        compiler_params=_params(1),
        name="ffn_ln",
    )(x, w_gate, w_up, w_down, gain, bias, *[stack for stack, _ in convert])
    return outs[0], outs[1:]


def _pool_ln_body(seq_len, xm_ref, xp_ref, xn_ref, win_ref, wgrp_ref, scale_ref, wout_ref,
                  gain_ref, bias_ref, conv_in_ref, o_ref, conv_out_ref, xe_ref, u_ref, lvl_ref):
    conv_out_ref[...] = conv_in_ref[...].astype(jnp.bfloat16)
    ts = xm_ref.shape[0]
    halo = POOL_HALO
    s = pl.program_id(1)
    n_s = pl.num_programs(1)
    xe_ref[0:halo, :] = xp_ref[...]
    xe_ref[halo:halo + ts, :] = xm_ref[...]
    xe_ref[halo + ts:, :] = xn_ref[...]
    tb = POOL_SUB_TILE
    n_ext = tb + 2 * halo
    n_sub = ts // tb
    for k in range(n_sub):
        r0 = k * tb
        u_ext = _dot(xe_ref[r0:r0 + n_ext, :].astype(jnp.bfloat16), win_ref[...])
        row = lax.broadcasted_iota(jnp.int32, (n_ext, 1), 0)
        if k == 0:
            u_ext = jnp.where((row < halo) & (s == 0), 0.0, u_ext)
        if k == n_sub - 1:
            u_ext = jnp.where((row >= halo + tb) & (s == n_s - 1), 0.0, u_ext)
        u_ref[k, 0:n_ext, :] = u_ext
        u_ref[k, n_ext:, :] = jnp.zeros((V7X_SUBLANES, D_MODEL), jnp.float32)

        t_abs = s * ts + r0 + lax.broadcasted_iota(jnp.int32, (tb, 1), 0)
        ys = []
        lvl = 0
        for g, w in enumerate(POOL_WINDOWS):
            half = w // 2
            cols = slice(g * POOL_GROUP_DIM, (g + 1) * POOL_GROUP_DIM)
            src = functools.partial(lambda k_, c_, lo, n: u_ref[k_, lo:lo + n, c_], k, cols)
            width = 1
            while 2 * width < w:
                lvl_ref[k, lvl, 0:n_ext, :] = src(0, n_ext) + src(width, n_ext)
                lvl_ref[k, lvl, n_ext:, :] = jnp.zeros((V7X_SUBLANES, POOL_GROUP_DIM), jnp.float32)
                src = functools.partial(lambda k_, l_, lo, n: lvl_ref[k_, l_, lo:lo + n, :], k, lvl)
                width, lvl = 2 * width, lvl + 1
            start = halo - half
            win_sum = src(start, tb) + src(start + width, tb)
            count = jnp.minimum(t_abs + half, seq_len) - jnp.maximum(t_abs - half, 0)
            mean = win_sum / count.astype(jnp.float32)
            mixed = (mean - u_ref[k, halo:halo + tb, cols]).astype(jnp.bfloat16)
            ys.append(_dot(mixed, wgrp_ref[g]))
        y = (jnp.concatenate(ys, axis=1) * scale_ref[...]).astype(jnp.bfloat16)
        m = _dot(y, wout_ref[...])
        o_ref[r0:r0 + tb, :] = _layer_norm(ALPHA * xm_ref[r0:r0 + tb, :] + m,
                                           gain_ref[...], bias_ref[...])


def _pool_ln(x, w_in, w_group, scale, w_out, gain, bias, convert):
    b, s_len, _ = x.shape
    ts = POOL_ROW_TILE
    halo = POOL_HALO
    blocks_per_tile = ts // halo
    n_halo_blocks = s_len // halo
    n_levels = sum(int(np.log2(w)) - 1 for w in POOL_WINDOWS)
    tb = POOL_SUB_TILE
    buf_rows = tb + 2 * halo + V7X_SUBLANES
    main_spec = pl.BlockSpec((None, ts, D_MODEL), lambda bi, si: (bi, si, 0))
    prev_spec = pl.BlockSpec(
        (None, halo, D_MODEL),
        lambda bi, si: (bi, jnp.maximum(si * blocks_per_tile - 1, 0), 0))
    next_spec = pl.BlockSpec(
        (None, halo, D_MODEL),
        lambda bi, si: (bi, jnp.minimum((si + 1) * blocks_per_tile, n_halo_blocks - 1), 0))
    n_s = s_len // ts
    conv_in, conv_out, conv_shapes = _convert_specs([convert], b * n_s, lambda bi, si: bi * n_s + si)
    outs = pl.pallas_call(
        functools.partial(_pool_ln_body, s_len),
        grid=(b, n_s),
        in_specs=[main_spec, prev_spec, next_spec,
                  _const_spec((D_MODEL, D_MODEL)),
                  _const_spec((N_POOL_GROUPS, POOL_GROUP_DIM, POOL_GROUP_DIM)),
                  _const_spec((1, D_MODEL)),
                  _const_spec((D_MODEL, D_MODEL)),
                  _const_spec((1, D_MODEL)), _const_spec((1, D_MODEL))] + conv_in,
        out_specs=[main_spec] + conv_out,
        out_shape=[jax.ShapeDtypeStruct(x.shape, jnp.float32)] + conv_shapes,
        scratch_shapes=[pltpu.VMEM((ts + 2 * halo, D_MODEL), jnp.float32),
                        pltpu.VMEM((ts // tb, buf_rows, D_MODEL), jnp.float32),
                        pltpu.VMEM((ts // tb, n_levels, buf_rows, POOL_GROUP_DIM), jnp.float32)],
        compiler_params=_params(2),
        name="pool_ln",
    )(x, x, x, w_in, w_group, scale, w_out, gain, bias, convert[0])
    return outs[0], outs[1:]


def _qkv_body(x_ref, w_ref, o0_ref, o1_ref, o2_ref, x_slabs):
    tm = x_ref.shape[0]
    n_slabs = D_MODEL // V7X_LANES
    for c in range(n_slabs):
        x_slabs[c] = x_ref[:, c * V7X_LANES:(c + 1) * V7X_LANES]
    ts = QKV_SUB_TILE
    for r0 in range(0, tm, ts):
        for g, ((_, d), o_ref) in enumerate(zip(DIL_CONFIGS, (o0_ref, o1_ref, o2_ref))):
            n = ts // d
            if d == 1:
                xp = x_ref[r0:r0 + ts, :]
            else:
                xp = jnp.concatenate(
                    [jnp.concatenate([x_slabs[c, pl.ds(r0 + r, n, stride=d), :] for r in range(d)],
                                     axis=0)
                     for c in range(n_slabs)], axis=1)
            xp = xp.astype(jnp.bfloat16)
            for j in range(3):
                lo = (g * 3 + j) * ATTN_WIDTH
                res = _dot(xp, w_ref[:, lo:lo + ATTN_WIDTH])
                if j == 0:
                    res = res * (HEAD_DIM ** -0.5 * LOG2_E)
                res = res.astype(jnp.bfloat16)
                for r in range(d):
                    o_ref[j, r, r0 // d:r0 // d + n, :] = res[r * n:(r + 1) * n]


def _qkv_proj(x, w_qkv, batch, seq_len):
    tm = QKV_ROW_TILE
    tiles_per_seq = seq_len // tm
    return pl.pallas_call(
        _qkv_body,
        grid=(batch * tiles_per_seq,),
        in_specs=[pl.BlockSpec((tm, D_MODEL), lambda i: (i, 0)),
                  _const_spec((D_MODEL, QKV_WIDTH))],
        out_specs=[pl.BlockSpec((3, None, d, tm // d, ATTN_WIDTH),
                                lambda i: (0, i // tiles_per_seq, 0, i % tiles_per_seq, 0))
                   for _, d in DIL_CONFIGS],
        out_shape=[jax.ShapeDtypeStruct((3, batch, d, seq_len // d, ATTN_WIDTH), jnp.bfloat16)
                   for _, d in DIL_CONFIGS],
        scratch_shapes=[pltpu.VMEM((D_MODEL // V7X_LANES, tm, V7X_LANES), jnp.float32)],
        compiler_params=_params(1),
        name="qkv_proj",
    )(x, w_qkv)


def _attn_bias(group):
    window, d = DIL_CONFIGS[group]
    qb, kb = ATTN_Q_BLOCK, ATTN_K_BLOCK
    r = (kb - qb) // 2
    a = np.arange(qb)[:, None]
    c = np.arange(kb)[None, :]
    rel = c - r - a
    band = np.abs(rel) <= r
    valid = np.stack([band, band & (c >= r), band & (c < qb + r)])
    neg_dist = np.where(valid, -(d * np.abs(rel)).astype(np.float64), MASK_VALUE).astype(np.float32)
    slopes = (_alibi_slopes()[group] * LOG2_E).astype(np.float32)
    return jnp.asarray(slopes)[None, :, None, None] * jnp.asarray(neg_dist)[:, None]


def _attn_body(n_blocks_total, q_ref, k_ref, kp_ref, kn_ref, v_ref, vp_ref, vn_ref, bias_ref,
               o_ref, stat_ref, ke_ref, ve_ref, s_ref, p_ref):
    n_res, tq, _ = q_ref.shape
    qb, kb = ATTN_Q_BLOCK, ATTN_K_BLOCK
    r = (kb - qb) // 2
    hq = qb // 2
    n_blk = tq // qb
    ti = pl.program_id(2)
    ke_ref[:, 0:r, :] = kp_ref[...]
    ke_ref[:, r:r + tq, :] = k_ref[...]
    ke_ref[:, r + tq:, :] = kn_ref[...]
    for pair in range(N_HEADS // 2):
        src = slice(pair * V7X_LANES, (pair + 1) * V7X_LANES)
        dst = slice(2 * pair * V7X_LANES, (2 * pair + 1) * V7X_LANES)
        one = slice((2 * pair + 1) * V7X_LANES, (2 * pair + 2) * V7X_LANES)
        ve_ref[:, 0:r, dst] = vp_ref[:, :, src]
        ve_ref[:, r:r + tq, dst] = v_ref[:, :, src]
        ve_ref[:, r + tq:, dst] = vn_ref[:, :, src]
        ve_ref[:, :, one] = jnp.ones((n_res, tq + 2 * r, V7X_LANES), jnp.bfloat16)

    low_half = lax.broadcasted_iota(jnp.int32, (qb, V7X_LANES), 1) < HEAD_DIM

    for res, blk in [(a, b) for a in range(n_res) for b in range(n_blk)]:
        row0 = blk * qb
        g_blk = ti * n_blk + blk
        variant = jnp.where(g_blk == 0, 1, 0) + jnp.where(g_blk == n_blocks_total - 1, 2, 0)
        stat_ref[res, pl.ds(row0, qb), :] = jnp.zeros((qb, STAT_WIDTH), jnp.float32)
        for pair in range(N_HEADS // 2):
            cols = slice(pair * V7X_LANES, (pair + 1) * V7X_LANES)
            buf = pair % ATTN_SCORE_BUFFERS
            q2 = q_ref[res, pl.ds(row0, qb), cols]
            zero = jnp.zeros_like(q2)
            q_stack = jnp.concatenate(
                [jnp.where(low_half, q2, zero), jnp.where(low_half, zero, q2)], axis=0)
            s_ref[buf] = lax.dot_general(q_stack, ke_ref[res, pl.ds(row0, kb), cols],
                                         (((1,), (1,)), ((), ())),
                                         preferred_element_type=jnp.float32)
            for c in range(4):
                head, half = 2 * pair + c // 2, c % 2
                rows = slice(c * hq, (c + 1) * hq)
                sc = s_ref[buf, rows, :] + bias_ref[variant, head, half * hq:(half + 1) * hq, :]
                m = jnp.max(sc, axis=1, keepdims=True)
                p_ref[buf, rows, :] = jnp.exp2(sc - m).astype(jnp.bfloat16)
                stat_ref[res, pl.ds(row0 + half * hq, hq), head:head + 1] = m
            v_cols = slice(2 * pair * V7X_LANES, (2 * pair + 2) * V7X_LANES)
            o2 = _dot(p_ref[buf], ve_ref[res, pl.ds(row0, kb), v_cols])
            o_ref[res, pl.ds(row0, qb), cols] = jnp.where(
                low_half, o2[:qb, :V7X_LANES], o2[qb:, :V7X_LANES]).astype(o_ref.dtype)
            for hh in range(2):
                lane_l = N_HEADS + 2 * pair + hh
                stat_ref[res, pl.ds(row0, qb), lane_l:lane_l + 1] = (
                    o2[hh * qb:(hh + 1) * qb, V7X_LANES + lane_l:V7X_LANES + lane_l + 1])


def _attn_group(qkv_g, group):
    window, d = DIL_CONFIGS[group]
    _, batch, _, sub_len, _ = qkv_g.shape
    qb, kb = ATTN_Q_BLOCK, ATTN_K_BLOCK
    r = window // (2 * d)
    assert r == (kb - qb) // 2 and sub_len >= 2 * qb
    tq = min(ATTN_Q_TILE, sub_len)
    n_tiles = sub_len // tq
    n_res = min(d, ATTN_Q_TILE // tq)
    halo_per_tile = tq // r
    n_halo_blocks = sub_len // r

    def main_spec(j):
        return pl.BlockSpec((None, None, n_res, tq, ATTN_WIDTH),
                            lambda bi, ri, ti: (j, bi, ri, ti, 0))

    def prev_spec(j):
        return pl.BlockSpec(
            (None, None, n_res, r, ATTN_WIDTH),
            lambda bi, ri, ti: (j, bi, ri, jnp.maximum(ti * halo_per_tile - 1, 0), 0))

    def next_spec(j):
        return pl.BlockSpec(
            (None, None, n_res, r, ATTN_WIDTH),
            lambda bi, ri, ti: (j, bi, ri, jnp.minimum((ti + 1) * halo_per_tile, n_halo_blocks - 1),
                                0))

    def out_spec(width):
        return pl.BlockSpec((None, n_res, tq, width), lambda bi, ri, ti: (bi, ri, ti, 0))

    return pl.pallas_call(
        functools.partial(_attn_body, sub_len // qb),
        grid=(batch, d // n_res, n_tiles),
        in_specs=[main_spec(0),
                  main_spec(1), prev_spec(1), next_spec(1),
                  main_spec(2), prev_spec(2), next_spec(2),
                  _const_spec((3, N_HEADS, qb, kb))],
        out_specs=[out_spec(ATTN_WIDTH), out_spec(STAT_WIDTH)],
        out_shape=[jax.ShapeDtypeStruct((batch, d, sub_len, ATTN_WIDTH), jnp.bfloat16),
                   jax.ShapeDtypeStruct((batch, d, sub_len, STAT_WIDTH), jnp.float32)],
        scratch_shapes=[pltpu.VMEM((n_res, tq + 2 * r, ATTN_WIDTH), jnp.bfloat16),
                        pltpu.VMEM((n_res, tq + 2 * r, 2 * ATTN_WIDTH), jnp.bfloat16),
                        pltpu.VMEM((ATTN_SCORE_BUFFERS, 2 * qb, kb), jnp.float32),
                        pltpu.VMEM((ATTN_SCORE_BUFFERS, 2 * qb, kb), jnp.bfloat16)],
        compiler_params=_params(3),
        name=f"attn_group{group}",
    )(*([qkv_g] * 7), _attn_bias(group))


def _head_expand_matrix():
    k = np.arange(2 * STAT_WIDTH)[:, None] % STAT_WIDTH
    c = np.arange(ATTN_WIDTH)[None, :] // HEAD_DIM
    return jnp.asarray(k == c, dtype=jnp.bfloat16)


def _attn_out_ln_body(x_ref, o0_ref, o1_ref, o2_ref, l0_ref, l1_ref, l2_ref, e_ref, w_ref,
                      gain_ref, bias_ref, out_ref, o_scr, l_scr):
    o_refs = (o0_ref, o1_ref, o2_ref)
    l_refs = (l0_ref, l1_ref, l2_ref)

    def token_order(ref, scr, g, r0, ts):
        assert DIL_CONFIGS[0][1] == 1 and all(dil > 1 for _, dil in DIL_CONFIGS[1:])
        d, _, width = ref.shape
        if d == 1:
            return ref[0, r0:r0 + ts, :].astype(jnp.float32)
        n_slabs = width // V7X_LANES
        for r in range(d):
            piece = ref[r, r0 // d:(r0 + ts) // d, :].astype(jnp.float32)
            for c in range(n_slabs):
                scr[g - 1, c, pl.ds(r0 + r, ts // d, stride=d), :] = (
                    piece[:, c * V7X_LANES:(c + 1) * V7X_LANES])
        return jnp.concatenate([scr[g - 1, c, r0:r0 + ts, :] for c in range(n_slabs)], axis=1)

    ts = OUT_SUB_TILE
    is_head = lax.broadcasted_iota(jnp.int32, (ts, STAT_WIDTH), 1) < N_HEADS
    for r0 in range(0, x_ref.shape[0], ts):
        ms = [token_order(l_refs[g], l_scr, g, r0, ts) for g in range(N_DIL_GROUPS)]
        ls = [pltpu.roll(m, STAT_WIDTH - N_HEADS, 1) for m in ms]
        top = jnp.maximum(jnp.maximum(ms[0], ms[1]), ms[2])
        es = [jnp.exp2(m - top) for m in ms]
        inv = 1.0 / (es[0] * ls[0] + es[1] * ls[1] + es[2] * ls[2])
        acc = None
        for g in range(N_DIL_GROUPS):
            wt = jnp.where(is_head, es[g] * inv, 0.0)
            hi = wt.astype(jnp.bfloat16)
            lo = (wt - hi.astype(jnp.float32)).astype(jnp.bfloat16)
            w_full = _dot(jnp.concatenate([hi, lo], axis=1), e_ref[...])
            term = w_full * token_order(o_refs[g], o_scr, g, r0, ts)
            acc = term if acc is None else acc + term
        m = _dot(acc.astype(jnp.bfloat16), w_ref[...])
        out_ref[r0:r0 + ts, :] = _layer_norm(ALPHA * x_ref[r0:r0 + ts, :] + m,
                                             gain_ref[...], bias_ref[...])


def _attn_out_ln(x, os_, stats, w_out, gain, bias, seq_len):
    m = x.shape[0]
    tm = OUT_ROW_TILE
    tiles_per_seq = seq_len // tm
    row_spec = pl.BlockSpec((tm, D_MODEL), lambda i: (i, 0))

    def group_spec(d, width):
        return pl.BlockSpec((None, d, tm // d, width),
                            lambda i: (i // tiles_per_seq, 0, i % tiles_per_seq, 0))

    return pl.pallas_call(
        _attn_out_ln_body,
        grid=(m // tm,),
        in_specs=[row_spec]
                 + [group_spec(d, ATTN_WIDTH) for _, d in DIL_CONFIGS]
                 + [group_spec(d, STAT_WIDTH) for _, d in DIL_CONFIGS]
                 + [_const_spec((2 * STAT_WIDTH, ATTN_WIDTH)),
                    _const_spec((ATTN_WIDTH, D_MODEL)),
                    _const_spec((1, D_MODEL)), _const_spec((1, D_MODEL))],
        out_specs=row_spec,
        out_shape=jax.ShapeDtypeStruct((m, D_MODEL), jnp.float32),
        scratch_shapes=[
            pltpu.VMEM((N_DIL_GROUPS - 1, ATTN_WIDTH // V7X_LANES, tm, V7X_LANES), jnp.float32),
            pltpu.VMEM((N_DIL_GROUPS - 1, STAT_WIDTH // V7X_LANES, tm, V7X_LANES), jnp.float32)],
        compiler_params=_params(1),
        name="attn_out_ln",
    )(x, *os_, *stats, _head_expand_matrix(), w_out, gain, bias)


def kernel(x, ffn1_w_gate, ffn1_w_up, ffn1_w_down, ffn2_w_gate, ffn2_w_up, ffn2_w_down,
           ln_gain, ln_bias, pool_w_in, pool_w_group, pool_scale, pool_w_out,
           attn_w_qkv, attn_w_out):
    batch, seq_len, _ = x.shape
    bf16 = jnp.bfloat16
    n_mixers = 2
    h = x.reshape(batch * seq_len, D_MODEL)

    def ln_params(i, j):
        return ln_gain[i, j].reshape(1, D_MODEL), ln_bias[i, j].reshape(1, D_MODEL)

    ffn_stacks = [(ffn1_w_gate, ffn1_w_up, ffn1_w_down), (ffn2_w_gate, ffn2_w_up, ffn2_w_down)]
    ffn_order = [(which, i) for i in range(DEPTH) for which in range(2)]
    ffn_ready = [None]

    def ffn(k, h, gain, bias):
        which, layer = ffn_order[k]
        convert = ()
        if k + 1 < len(ffn_order):
            nxt_which, nxt_layer = ffn_order[k + 1]
            convert = [(w, nxt_layer) for w in ffn_stacks[nxt_which]]
        if ffn_ready[0] is None:
            h, ffn_ready[0] = _ffn_ln(h, *ffn_stacks[which], gain, bias, layer=layer, convert=convert)
        else:
            h, ffn_ready[0] = _ffn_ln(h, *ffn_ready[0], gain, bias, convert=convert)
        return h

    w_qkv_ready = {}
    for i in range(DEPTH):
        h = ffn(2 * i, h, *ln_params(i, 0))
        if i % n_mixers == 0:
            p = i // n_mixers
            assert i + 1 < DEPTH
            h, (w_qkv_ready[(i + 1) // n_mixers],) = _pool_ln(
                h.reshape(batch, seq_len, D_MODEL), pool_w_in[p].astype(bf16),
                pool_w_group[p].astype(bf16), pool_scale[p].reshape(1, D_MODEL),
                pool_w_out[p].astype(bf16), *ln_params(i, 1), (attn_w_qkv, (i + 1) // n_mixers))
            h = h.reshape(batch * seq_len, D_MODEL)
        else:
            a = i // n_mixers
            qkv = _qkv_proj(h, w_qkv_ready[a], batch, seq_len)
            outs = [_attn_group(qkv[g], g) for g in range(N_DIL_GROUPS)]
            h = _attn_out_ln(h, [o for o, _ in outs], [l for _, l in outs],
                             attn_w_out[a].astype(bf16), *ln_params(i, 1), seq_len)
        h = ffn(2 * i + 1, h, *ln_params(i, 2))
    return h.reshape(batch, seq_len, D_MODEL)
```

```python
import functools

import numpy as np
import jax
import jax.numpy as jnp
from jax import lax
from jax.experimental import pallas as pl
from jax.experimental.pallas import tpu as pltpu

D_MODEL = 1024
DEPTH = 2
POOL_WINDOWS = (2, 4, 8, 16)
N_POOL_GROUPS = len(POOL_WINDOWS)
POOL_GROUP_DIM = D_MODEL // N_POOL_GROUPS
POOL_HALO = max(POOL_WINDOWS) // 2
HEAD_DIM = 64
N_HEADS = D_MODEL // HEAD_DIM
DIL_CONFIGS = ((128, 1), (512, 4), (2048, 16))
N_DIL_GROUPS = len(DIL_CONFIGS)
ATTN_WIDTH = N_HEADS * HEAD_DIM
QKV_WIDTH = N_DIL_GROUPS * 3 * ATTN_WIDTH
D_FF = 2816
MACARON_WEIGHT = 0.5
ALPHA = (2.0 * DEPTH) ** 0.25
LN_EPS = 1e-5
MASK_VALUE = -1e30
LOG2_E = float(np.log2(np.e))

V7X_LANES = 128
V7X_MXU_DIM = 256
V7X_SUBLANES = 8
V7X_BF16_SUBLANES = 16
V7X_VMEM_LIMIT_BYTES = 56 * 1024 * 1024

FFN_ROW_TILE = 1024
FFN_SUB_TILE = 256
FFN_CHUNKS = ((0, 6 * V7X_MXU_DIM), (6 * V7X_MXU_DIM, D_FF))
WEIGHT_LOAD_CHUNKS = 8
POOL_ROW_TILE = 1024
POOL_SUB_TILE = 256
QKV_ROW_TILE = 512
QKV_SUB_TILE = 512
ATTN_Q_TILE = 1024
ATTN_Q_BLOCK = 128
ATTN_K_BLOCK = 256
ATTN_SCORE_BUFFERS = 2
OUT_ROW_TILE = 1024
OUT_SUB_TILE = 512
STAT_WIDTH = V7X_LANES


def _alibi_slopes():
    n = N_DIL_GROUPS * N_HEADS
    s = 2.0 ** (-8.0 * np.arange(1, n + 1) / n)
    return s.reshape(N_DIL_GROUPS, N_HEADS)


def _layer_norm(y, gain, bias):
    mu = jnp.mean(y, axis=-1, keepdims=True)
    yc = y - mu
    var = jnp.mean(yc * yc, axis=-1, keepdims=True)
    return yc * lax.rsqrt(var + LN_EPS) * gain + bias


def _dot(a, b):
    return jnp.dot(a, b, preferred_element_type=jnp.float32)


def _const_spec(shape):
    return pl.BlockSpec(shape, lambda *_: (0,) * len(shape), pipeline_mode=pl.Buffered(1))


def _params(n_grid_dims):
    return pltpu.CompilerParams(
        dimension_semantics=("arbitrary",) * n_grid_dims,
        vmem_limit_bytes=V7X_VMEM_LIMIT_BYTES)


def _load_weight_bf16(src_hbm, dst_ref, stage_ref, sem):
    n_rows = stage_ref.shape[1]
    n_chunks = src_hbm.shape[0] // n_rows
    assert n_chunks * n_rows == src_hbm.shape[0]

    def chunk_copy(c):
        return pltpu.make_async_copy(src_hbm.at[pl.ds(c * n_rows, n_rows), :],
                                     stage_ref.at[c % 2], sem.at[c % 2])

    chunk_copy(0).start()
    for c in range(n_chunks):
        if c + 1 < n_chunks:
            chunk_copy(c + 1).start()
        chunk_copy(c).wait()
        dst_ref[c * n_rows:(c + 1) * n_rows, :] = stage_ref[c % 2].astype(jnp.bfloat16)


def _convert_specs(convert, n_steps, step_index):
    in_specs, out_specs, out_shapes = [], [], []
    for stack, layer in convert:
        _, rows, cols = stack.shape
        blk = rows // n_steps
        assert blk * n_steps == rows and blk % V7X_BF16_SUBLANES == 0
        in_specs.append(pl.BlockSpec(
            (None, blk, cols), lambda *ids, layer=layer: (layer, step_index(*ids), 0)))
        out_specs.append(pl.BlockSpec((blk, cols), lambda *ids: (step_index(*ids), 0)))
        out_shapes.append(jax.ShapeDtypeStruct((rows, cols), jnp.bfloat16))
    return in_specs, out_specs, out_shapes


def _ffn_ln_body(layer, n_conv, *refs):
    x_ref, wg_in, wu_in, wd_in, gain_ref, bias_ref = refs[:6]
    conv_in = refs[6:6 + n_conv]
    o_ref = refs[6 + n_conv]
    conv_out = refs[7 + n_conv:7 + 2 * n_conv]
    scratch = refs[7 + 2 * n_conv:]
    if layer is None:
        wg_ref, wu_ref, wd_ref = wg_in, wu_in, wd_in
    else:
        wg_ref, wu_ref, wd_ref, stage_in_ref, stage_down_ref, sem = scratch

        @pl.when(pl.program_id(0) == 0)
        def _():
            _load_weight_bf16(wg_in.at[layer], wg_ref, stage_in_ref, sem)
            _load_weight_bf16(wu_in.at[layer], wu_ref, stage_in_ref, sem)
            _load_weight_bf16(wd_in.at[layer], wd_ref, stage_down_ref, sem)

    for src, dst in zip(conv_in, conv_out):
        dst[...] = src[...].astype(jnp.bfloat16)

    for r0 in range(0, x_ref.shape[0], FFN_SUB_TILE):
        rows = slice(r0, r0 + FFN_SUB_TILE)
        x = x_ref[rows, :]
        xb = x.astype(jnp.bfloat16)
        h = None
        for lo, hi in FFN_CHUNKS:
            g = _dot(xb, wg_ref[:, lo:hi])
            u = _dot(xb, wu_ref[:, lo:hi])
            hg = 0.5 * g
            a = ((hg + hg * jnp.tanh(hg)) * u).astype(jnp.bfloat16)
            part = _dot(a, wd_ref[lo:hi, :])
            h = part if h is None else h + part
        y = ALPHA * x + MACARON_WEIGHT * h
        o_ref[rows, :] = _layer_norm(y, gain_ref[...], bias_ref[...])


def _ffn_ln(x, w_gate, w_up, w_down, gain, bias, layer=None, convert=()):
    m = x.shape[0]
    tm = FFN_ROW_TILE
    n_steps = m // tm
    row_spec = pl.BlockSpec((tm, D_MODEL), lambda i: (i, 0))
    if layer is None:
        weight_specs = [_const_spec((D_MODEL, D_FF)), _const_spec((D_MODEL, D_FF)),
                        _const_spec((D_FF, D_MODEL))]
        scratch = []
    else:
        weight_specs = [pl.BlockSpec(memory_space=pl.ANY)] * 3
        scratch = [pltpu.VMEM((D_MODEL, D_FF), jnp.bfloat16),
                   pltpu.VMEM((D_MODEL, D_FF), jnp.bfloat16),
                   pltpu.VMEM((D_FF, D_MODEL), jnp.bfloat16),
                   pltpu.VMEM((2, D_MODEL // WEIGHT_LOAD_CHUNKS, D_FF), jnp.float32),
                   pltpu.VMEM((2, D_FF // WEIGHT_LOAD_CHUNKS, D_MODEL), jnp.float32),
                   pltpu.SemaphoreType.DMA((2,))]
    conv_in, conv_out, conv_shapes = _convert_specs(convert, n_steps, lambda i: i)
    outs = pl.pallas_call(
        functools.partial(_ffn_ln_body, layer, len(convert)),
        grid=(n_steps,),
        in_specs=[row_spec] + weight_specs
                 + [_const_spec((1, D_MODEL)), _const_spec((1, D_MODEL))] + conv_in,
        out_specs=[row_spec] + conv_out,
        out_shape=[jax.ShapeDtypeStruct((m, D_MODEL), jnp.float32)] + conv_shapes,
        scratch_shapes=scratch,
        compiler_params=_params(1),
        name="ffn_ln",
    )(x, w_gate, w_up, w_down, gain, bias, *[stack for stack, _ in convert])
    return outs[0], outs[1:]


def _pool_ln_body(seq_len, xm_ref, xp_ref, xn_ref, win_ref, wgrp_ref, scale_ref, wout_ref,
                  gain_ref, bias_ref, conv_in_ref, o_ref, conv_out_ref, xe_ref, u_ref, lvl_ref):
    conv_out_ref[...] = conv_in_ref[...].astype(jnp.bfloat16)
    ts = xm_ref.shape[0]
    halo = POOL_HALO
    s = pl.program_id(1)
    n_s = pl.num_programs(1)
    xe_ref[0:halo, :] = xp_ref[...]
    xe_ref[halo:halo + ts, :] = xm_ref[...]
    xe_ref[halo + ts:, :] = xn_ref[...]
    tb = POOL_SUB_TILE
    n_ext = tb + 2 * halo
    n_sub = ts // tb
    for k in range(n_sub):
        r0 = k * tb
        u_ext = _dot(xe_ref[r0:r0 + n_ext, :].astype(jnp.bfloat16), win_ref[...])
        row = lax.broadcasted_iota(jnp.int32, (n_ext, 1), 0)
        if k == 0:
            u_ext = jnp.where((row < halo) & (s == 0), 0.0, u_ext)
        if k == n_sub - 1:
            u_ext = jnp.where((row >= halo + tb) & (s == n_s - 1), 0.0, u_ext)
        u_ref[k, 0:n_ext, :] = u_ext
        u_ref[k, n_ext:, :] = jnp.zeros((V7X_SUBLANES, D_MODEL), jnp.float32)

        t_abs = s * ts + r0 + lax.broadcasted_iota(jnp.int32, (tb, 1), 0)
        ys = []
        lvl = 0
        for g, w in enumerate(POOL_WINDOWS):
            half = w // 2
            cols = slice(g * POOL_GROUP_DIM, (g + 1) * POOL_GROUP_DIM)
            src = functools.partial(lambda k_, c_, lo, n: u_ref[k_, lo:lo + n, c_], k, cols)
            width = 1
            while 2 * width < w:
                lvl_ref[k, lvl, 0:n_ext, :] = src(0, n_ext) + src(width, n_ext)
                lvl_ref[k, lvl, n_ext:, :] = jnp.zeros((V7X_SUBLANES, POOL_GROUP_DIM), jnp.float32)
                src = functools.partial(lambda k_, l_, lo, n: lvl_ref[k_, l_, lo:lo + n, :], k, lvl)
                width, lvl = 2 * width, lvl + 1
            start = halo - half
            win_sum = src(start, tb) + src(start + width, tb)
            count = jnp.minimum(t_abs + half, seq_len) - jnp.maximum(t_abs - half, 0)
            mean = win_sum / count.astype(jnp.float32)
            mixed = (mean - u_ref[k, halo:halo + tb, cols]).astype(jnp.bfloat16)
            ys.append(_dot(mixed, wgrp_ref[g]))
        y = (jnp.concatenate(ys, axis=1) * scale_ref[...]).astype(jnp.bfloat16)
        m = _dot(y, wout_ref[...])
        o_ref[r0:r0 + tb, :] = _layer_norm(ALPHA * xm_ref[r0:r0 + tb, :] + m,
                                           gain_ref[...], bias_ref[...])


def _pool_ln(x, w_in, w_group, scale, w_out, gain, bias, convert):
    b, s_len, _ = x.shape
    ts = POOL_ROW_TILE
    halo = POOL_HALO
    blocks_per_tile = ts // halo
    n_halo_blocks = s_len // halo
    n_levels = sum(int(np.log2(w)) - 1 for w in POOL_WINDOWS)
    tb = POOL_SUB_TILE
    buf_rows = tb + 2 * halo + V7X_SUBLANES
    main_spec = pl.BlockSpec((None, ts, D_MODEL), lambda bi, si: (bi, si, 0))
    prev_spec = pl.BlockSpec(
        (None, halo, D_MODEL),
        lambda bi, si: (bi, jnp.maximum(si * blocks_per_tile - 1, 0), 0))
    next_spec = pl.BlockSpec(
        (None, halo, D_MODEL),
        lambda bi, si: (bi, jnp.minimum((si + 1) * blocks_per_tile, n_halo_blocks - 1), 0))
    n_s = s_len // ts
    conv_in, conv_out, conv_shapes = _convert_specs([convert], b * n_s, lambda bi, si: bi * n_s + si)
    outs = pl.pallas_call(
        functools.partial(_pool_ln_body, s_len),
        grid=(b, n_s),
        in_specs=[main_spec, prev_spec, next_spec,
                  _const_spec((D_MODEL, D_MODEL)),
                  _const_spec((N_POOL_GROUPS, POOL_GROUP_DIM, POOL_GROUP_DIM)),
                  _const_spec((1, D_MODEL)),
                  _const_spec((D_MODEL, D_MODEL)),
                  _const_spec((1, D_MODEL)), _const_spec((1, D_MODEL))] + conv_in,
        out_specs=[main_spec] + conv_out,
        out_shape=[jax.ShapeDtypeStruct(x.shape, jnp.float32)] + conv_shapes,
        scratch_shapes=[pltpu.VMEM((ts + 2 * halo, D_MODEL), jnp.float32),
                        pltpu.VMEM((ts // tb, buf_rows, D_MODEL), jnp.float32),
                        pltpu.VMEM((ts // tb, n_levels, buf_rows, POOL_GROUP_DIM), jnp.float32)],
        compiler_params=_params(2),
        name="pool_ln",
    )(x, x, x, w_in, w_group, scale, w_out, gain, bias, convert[0])
    return outs[0], outs[1:]


def _qkv_body(x_ref, w_ref, o0_ref, o1_ref, o2_ref, x_slabs):
    tm = x_ref.shape[0]
    n_slabs = D_MODEL // V7X_LANES
    for c in range(n_slabs):
        x_slabs[c] = x_ref[:, c * V7X_LANES:(c + 1) * V7X_LANES]
    ts = QKV_SUB_TILE
    for r0 in range(0, tm, ts):
        for g, ((_, d), o_ref) in enumerate(zip(DIL_CONFIGS, (o0_ref, o1_ref, o2_ref))):
            n = ts // d
            if d == 1:
                xp = x_ref[r0:r0 + ts, :]
            else:
                xp = jnp.concatenate(
                    [jnp.concatenate([x_slabs[c, pl.ds(r0 + r, n, stride=d), :] for r in range(d)],
                                     axis=0)
                     for c in range(n_slabs)], axis=1)
            xp = xp.astype(jnp.bfloat16)
            for j in range(3):
                lo = (g * 3 + j) * ATTN_WIDTH
                res = _dot(xp, w_ref[:, lo:lo + ATTN_WIDTH])
                if j == 0:
                    res = res * (HEAD_DIM ** -0.5 * LOG2_E)
                res = res.astype(jnp.bfloat16)
                for r in range(d):
                    o_ref[j, r, r0 // d:r0 // d + n, :] = res[r * n:(r + 1) * n]


def _qkv_proj(x, w_qkv, batch, seq_len):
    tm = QKV_ROW_TILE
    tiles_per_seq = seq_len // tm
    return pl.pallas_call(
        _qkv_body,
        grid=(batch * tiles_per_seq,),
        in_specs=[pl.BlockSpec((tm, D_MODEL), lambda i: (i, 0)),
                  _const_spec((D_MODEL, QKV_WIDTH))],
        out_specs=[pl.BlockSpec((3, None, d, tm // d, ATTN_WIDTH),
                                lambda i: (0, i // tiles_per_seq, 0, i % tiles_per_seq, 0))
                   for _, d in DIL_CONFIGS],
        out_shape=[jax.ShapeDtypeStruct((3, batch, d, seq_len // d, ATTN_WIDTH), jnp.bfloat16)
                   for _, d in DIL_CONFIGS],
        scratch_shapes=[pltpu.VMEM((D_MODEL // V7X_LANES, tm, V7X_LANES), jnp.float32)],
        compiler_params=_params(1),
        name="qkv_proj",
    )(x, w_qkv)


def _attn_bias(group):
    window, d = DIL_CONFIGS[group]
    qb, kb = ATTN_Q_BLOCK, ATTN_K_BLOCK
    r = (kb - qb) // 2
    a = np.arange(qb)[:, None]
    c = np.arange(kb)[None, :]
    rel = c - r - a
    band = np.abs(rel) <= r
    valid = np.stack([band, band & (c >= r), band & (c < qb + r)])
    neg_dist = np.where(valid, -(d * np.abs(rel)).astype(np.float64), MASK_VALUE).astype(np.float32)
    slopes = (_alibi_slopes()[group] * LOG2_E).astype(np.float32)
    return jnp.asarray(slopes[None, :, None, None] * neg_dist[:, None])


def _attn_body(n_blocks_total, q_ref, k_ref, kp_ref, kn_ref, v_ref, vp_ref, vn_ref, bias_ref,
               o_ref, stat_ref, ke_ref, ve_ref, s_ref, p_ref):
    n_res, tq, _ = q_ref.shape
    qb, kb = ATTN_Q_BLOCK, ATTN_K_BLOCK
    r = (kb - qb) // 2
    hq = qb // 2
    n_blk = tq // qb
    ti = pl.program_id(2)
    ke_ref[:, 0:r, :] = kp_ref[...]
    ke_ref[:, r:r + tq, :] = k_ref[...]
    ke_ref[:, r + tq:, :] = kn_ref[...]
    for pair in range(N_HEADS // 2):
        src = slice(pair * V7X_LANES, (pair + 1) * V7X_LANES)
        dst = slice(2 * pair * V7X_LANES, (2 * pair + 1) * V7X_LANES)
        one = slice((2 * pair + 1) * V7X_LANES, (2 * pair + 2) * V7X_LANES)
        ve_ref[:, 0:r, dst] = vp_ref[:, :, src]
        ve_ref[:, r:r + tq, dst] = v_ref[:, :, src]
        ve_ref[:, r + tq:, dst] = vn_ref[:, :, src]
        ve_ref[:, :, one] = jnp.ones((n_res, tq + 2 * r, V7X_LANES), jnp.bfloat16)

    low_half = lax.broadcasted_iota(jnp.int32, (qb, V7X_LANES), 1) < HEAD_DIM

    for res, blk in [(a, b) for a in range(n_res) for b in range(n_blk)]:
        row0 = blk * qb
        g_blk = ti * n_blk + blk
        variant = jnp.where(g_blk == 0, 1, 0) + jnp.where(g_blk == n_blocks_total - 1, 2, 0)
        stat_ref[res, pl.ds(row0, qb), :] = jnp.zeros((qb, STAT_WIDTH), jnp.float32)
        for pair in range(N_HEADS // 2):
            cols = slice(pair * V7X_LANES, (pair + 1) * V7X_LANES)
            buf = pair % ATTN_SCORE_BUFFERS
            q2 = q_ref[res, pl.ds(row0, qb), cols]
            zero = jnp.zeros_like(q2)
            q_stack = jnp.concatenate(
                [jnp.where(low_half, q2, zero), jnp.where(low_half, zero, q2)], axis=0)
            s_ref[buf] = lax.dot_general(q_stack, ke_ref[res, pl.ds(row0, kb), cols],
                                         (((1,), (1,)), ((), ())),
                                         preferred_element_type=jnp.float32)
            for c in range(4):
                head, half = 2 * pair + c // 2, c % 2
                rows = slice(c * hq, (c + 1) * hq)
                sc = s_ref[buf, rows, :] + bias_ref[variant, head, half * hq:(half + 1) * hq, :]
                m = jnp.max(sc, axis=1, keepdims=True)
                p_ref[buf, rows, :] = jnp.exp2(sc - m).astype(jnp.bfloat16)
                stat_ref[res, pl.ds(row0 + half * hq, hq), head:head + 1] = m
            v_cols = slice(2 * pair * V7X_LANES, (2 * pair + 2) * V7X_LANES)
            o2 = _dot(p_ref[buf], ve_ref[res, pl.ds(row0, kb), v_cols])
            o_ref[res, pl.ds(row0, qb), cols] = jnp.where(
                low_half, o2[:qb, :V7X_LANES], o2[qb:, :V7X_LANES]).astype(o_ref.dtype)
            for hh in range(2):
                lane_l = N_HEADS + 2 * pair + hh
                stat_ref[res, pl.ds(row0, qb), lane_l:lane_l + 1] = (
                    o2[hh * qb:(hh + 1) * qb, V7X_LANES + lane_l:V7X_LANES + lane_l + 1])


def _attn_group(qkv_g, group):
    window, d = DIL_CONFIGS[group]
    _, batch, _, sub_len, _ = qkv_g.shape
    qb, kb = ATTN_Q_BLOCK, ATTN_K_BLOCK
    r = window // (2 * d)
    assert r == (kb - qb) // 2 and sub_len >= 2 * qb
    tq = min(ATTN_Q_TILE, sub_len)
    n_tiles = sub_len // tq
    n_res = min(d, ATTN_Q_TILE // tq)
    halo_per_tile = tq // r
    n_halo_blocks = sub_len // r

    def main_spec(j):
        return pl.BlockSpec((None, None, n_res, tq, ATTN_WIDTH),
                            lambda bi, ri, ti: (j, bi, ri, ti, 0))

    def prev_spec(j):
        return pl.BlockSpec(
            (None, None, n_res, r, ATTN_WIDTH),
            lambda bi, ri, ti: (j, bi, ri, jnp.maximum(ti * halo_per_tile - 1, 0), 0))

    def next_spec(j):
        return pl.BlockSpec(
            (None, None, n_res, r, ATTN_WIDTH),
            lambda bi, ri, ti: (j, bi, ri, jnp.minimum((ti + 1) * halo_per_tile, n_halo_blocks - 1),
                                0))

    def out_spec(width):
        return pl.BlockSpec((None, n_res, tq, width), lambda bi, ri, ti: (bi, ri, ti, 0))

    return pl.pallas_call(
        functools.partial(_attn_body, sub_len // qb),
        grid=(batch, d // n_res, n_tiles),
        in_specs=[main_spec(0),
                  main_spec(1), prev_spec(1), next_spec(1),
                  main_spec(2), prev_spec(2), next_spec(2),
                  _const_spec((3, N_HEADS, qb, kb))],
        out_specs=[out_spec(ATTN_WIDTH), out_spec(STAT_WIDTH)],
        out_shape=[jax.ShapeDtypeStruct((batch, d, sub_len, ATTN_WIDTH), jnp.bfloat16),
                   jax.ShapeDtypeStruct((batch, d, sub_len, STAT_WIDTH), jnp.float32)],
        scratch_shapes=[pltpu.VMEM((n_res, tq + 2 * r, ATTN_WIDTH), jnp.bfloat16),
                        pltpu.VMEM((n_res, tq + 2 * r, 2 * ATTN_WIDTH), jnp.bfloat16),
                        pltpu.VMEM((ATTN_SCORE_BUFFERS, 2 * qb, kb), jnp.float32),
                        pltpu.VMEM((ATTN_SCORE_BUFFERS, 2 * qb, kb), jnp.bfloat16)],
        compiler_params=_params(3),
        name=f"attn_group{group}",
    )(*([qkv_g] * 7), _attn_bias(group))


def _head_expand_matrix():
    k = np.arange(2 * STAT_WIDTH)[:, None] % STAT_WIDTH
    c = np.arange(ATTN_WIDTH)[None, :] // HEAD_DIM
    return jnp.asarray(k == c, dtype=jnp.bfloat16)


def _attn_out_ln_body(x_ref, o0_ref, o1_ref, o2_ref, l0_ref, l1_ref, l2_ref, e_ref, w_ref,
                      gain_ref, bias_ref, out_ref, o_scr, l_scr):
    o_refs = (o0_ref, o1_ref, o2_ref)
    l_refs = (l0_ref, l1_ref, l2_ref)

    def token_order(ref, scr, g, r0, ts):
        assert DIL_CONFIGS[0][1] == 1 and all(dil > 1 for _, dil in DIL_CONFIGS[1:])
        d, _, width = ref.shape
        if d == 1:
            return ref[0, r0:r0 + ts, :].astype(jnp.float32)
        n_slabs = width // V7X_LANES
        for r in range(d):
            piece = ref[r, r0 // d:(r0 + ts) // d, :].astype(jnp.float32)
            for c in range(n_slabs):
                scr[g - 1, c, pl.ds(r0 + r, ts // d, stride=d), :] = (
                    piece[:, c * V7X_LANES:(c + 1) * V7X_LANES])
        return jnp.concatenate([scr[g - 1, c, r0:r0 + ts, :] for c in range(n_slabs)], axis=1)

    ts = OUT_SUB_TILE
    is_head = lax.broadcasted_iota(jnp.int32, (ts, STAT_WIDTH), 1) < N_HEADS
    for r0 in range(0, x_ref.shape[0], ts):
        ms = [token_order(l_refs[g], l_scr, g, r0, ts) for g in range(N_DIL_GROUPS)]
        ls = [pltpu.roll(m, STAT_WIDTH - N_HEADS, 1) for m in ms]
        top = jnp.maximum(jnp.maximum(ms[0], ms[1]), ms[2])
        es = [jnp.exp2(m - top) for m in ms]
        inv = 1.0 / (es[0] * ls[0] + es[1] * ls[1] + es[2] * ls[2])
        acc = None
        for g in range(N_DIL_GROUPS):
            wt = jnp.where(is_head, es[g] * inv, 0.0)
            hi = wt.astype(jnp.bfloat16)
            lo = (wt - hi.astype(jnp.float32)).astype(jnp.bfloat16)
            w_full = _dot(jnp.concatenate([hi, lo], axis=1), e_ref[...])
            term = w_full * token_order(o_refs[g], o_scr, g, r0, ts)
            acc = term if acc is None else acc + term
        m = _dot(acc.astype(jnp.bfloat16), w_ref[...])
        out_ref[r0:r0 + ts, :] = _layer_norm(ALPHA * x_ref[r0:r0 + ts, :] + m,
                                             gain_ref[...], bias_ref[...])


def _attn_out_ln(x, os_, stats, w_out, gain, bias, seq_len):
    m = x.shape[0]
    tm = OUT_ROW_TILE
    tiles_per_seq = seq_len // tm
    row_spec = pl.BlockSpec((tm, D_MODEL), lambda i: (i, 0))

    def group_spec(d, width):
        return pl.BlockSpec((None, d, tm // d, width),
                            lambda i: (i // tiles_per_seq, 0, i % tiles_per_seq, 0))

    return pl.pallas_call(
        _attn_out_ln_body,
        grid=(m // tm,),
        in_specs=[row_spec]
                 + [group_spec(d, ATTN_WIDTH) for _, d in DIL_CONFIGS]
                 + [group_spec(d, STAT_WIDTH) for _, d in DIL_CONFIGS]
                 + [_const_spec((2 * STAT_WIDTH, ATTN_WIDTH)),
                    _const_spec((ATTN_WIDTH, D_MODEL)),
                    _const_spec((1, D_MODEL)), _const_spec((1, D_MODEL))],
        out_specs=row_spec,
        out_shape=jax.ShapeDtypeStruct((m, D_MODEL), jnp.float32),
        scratch_shapes=[
            pltpu.VMEM((N_DIL_GROUPS - 1, ATTN_WIDTH // V7X_LANES, tm, V7X_LANES), jnp.float32),
            pltpu.VMEM((N_DIL_GROUPS - 1, STAT_WIDTH // V7X_LANES, tm, V7X_LANES), jnp.float32)],
        compiler_params=_params(1),
        name="attn_out_ln",
    )(x, *os_, *stats, _head_expand_matrix(), w_out, gain, bias)


def kernel(x, ffn1_w_gate, ffn1_w_up, ffn1_w_down, ffn2_w_gate, ffn2_w_up, ffn2_w_down,
           ln_gain, ln_bias, pool_w_in, pool_w_group, pool_scale, pool_w_out,
           attn_w_qkv, attn_w_out):
    batch, seq_len, _ = x.shape
    bf16 = jnp.bfloat16
    n_mixers = 2
    h = x.reshape(batch * seq_len, D_MODEL)

    def ln_params(i, j):
        return ln_gain[i, j].reshape(1, D_MODEL), ln_bias[i, j].reshape(1, D_MODEL)

    ffn_stacks = [(ffn1_w_gate, ffn1_w_up, ffn1_w_down), (ffn2_w_gate, ffn2_w_up, ffn2_w_down)]
    ffn_order = [(which, i) for i in range(DEPTH) for which in range(2)]
    ffn_ready = [None]

    def ffn(k, h, gain, bias):
        which, layer = ffn_order[k]
        convert = ()
        if k + 1 < len(ffn_order):
            nxt_which, nxt_layer = ffn_order[k + 1]
            convert = [(w, nxt_layer) for w in ffn_stacks[nxt_which]]
        if ffn_ready[0] is None:
            h, ffn_ready[0] = _ffn_ln(h, *ffn_stacks[which], gain, bias, layer=layer, convert=convert)
        else:
            h, ffn_ready[0] = _ffn_ln(h, *ffn_ready[0], gain, bias, convert=convert)
        return h

    w_qkv_ready = {}
    for i in range(DEPTH):
        h = ffn(2 * i, h, *ln_params(i, 0))
        if i % n_mixers == 0:
            p = i // n_mixers
            assert i + 1 < DEPTH
            h, (w_qkv_ready[(i + 1) // n_mixers],) = _pool_ln(
                h.reshape(batch, seq_len, D_MODEL), pool_w_in[p].astype(bf16),
                pool_w_group[p].astype(bf16), pool_scale[p].reshape(1, D_MODEL),
                pool_w_out[p].astype(bf16), *ln_params(i, 1), (attn_w_qkv, (i + 1) // n_mixers))
            h = h.reshape(batch * seq_len, D_MODEL)
        else:
            a = i // n_mixers
            qkv = _qkv_proj(h, w_qkv_ready[a], batch, seq_len)
            outs = [_attn_group(qkv[g], g) for g in range(N_DIL_GROUPS)]
            h = _attn_out_ln(h, [o for o, _ in outs], [l for _, l in outs],
                             attn_w_out[a].astype(bf16), *ln_params(i, 1), seq_len)
        h = ffn(2 * i + 1, h, *ln_params(i, 2))
    return h.reshape(batch, seq_len, D_MODEL)
```

```python
import functools

import numpy as np
import jax
import jax.numpy as jnp
from jax import lax
from jax.experimental import pallas as pl
from jax.experimental.pallas import tpu as pltpu

D_MODEL = 1024
DEPTH = 2
POOL_WINDOWS = (2, 4, 8, 16)
N_POOL_GROUPS = len(POOL_WINDOWS)
POOL_GROUP_DIM = D_MODEL // N_POOL_GROUPS
POOL_HALO = max(POOL_WINDOWS) // 2
HEAD_DIM = 64
N_HEADS = D_MODEL // HEAD_DIM
DIL_CONFIGS = ((128, 1), (512, 4), (2048, 16))
N_DIL_GROUPS = len(DIL_CONFIGS)
ATTN_WIDTH = N_HEADS * HEAD_DIM
QKV_WIDTH = N_DIL_GROUPS * 3 * ATTN_WIDTH
D_FF = 2816
MACARON_WEIGHT = 0.5
ALPHA = (2.0 * DEPTH) ** 0.25
LN_EPS = 1e-5
MASK_VALUE = -1e30
LOG2_E = float(np.log2(np.e))

V7X_LANES = 128
V7X_MXU_DIM = 256
V7X_SUBLANES = 8
V7X_BF16_SUBLANES = 16
V7X_VMEM_LIMIT_BYTES = 56 * 1024 * 1024

FFN_ROW_TILE = 1024
FFN_SUB_TILE = 256
FFN_CHUNKS = ((0, 6 * V7X_MXU_DIM), (6 * V7X_MXU_DIM, D_FF))
WEIGHT_LOAD_CHUNKS = 8
POOL_ROW_TILE = 1024
POOL_SUB_TILE = 256
QKV_ROW_TILE = 512
QKV_SUB_TILE = 512
ATTN_Q_TILE = 1024
ATTN_Q_BLOCK = 128
ATTN_K_BLOCK = 256
ATTN_SCORE_BUFFERS = 2
OUT_ROW_TILE = 1024
OUT_SUB_TILE = 512
STAT_WIDTH = V7X_LANES


def _alibi_slopes():
    n = N_DIL_GROUPS * N_HEADS
    s = 2.0 ** (-8.0 * np.arange(1, n + 1) / n)
    return s.reshape(N_DIL_GROUPS, N_HEADS)


def _layer_norm(y, gain, bias):
    mu = jnp.mean(y, axis=-1, keepdims=True)
    yc = y - mu
    var = jnp.mean(yc * yc, axis=-1, keepdims=True)
    return yc * lax.rsqrt(var + LN_EPS) * gain + bias


def _dot(a, b):
    return jnp.dot(a, b, preferred_element_type=jnp.float32)


def _const_spec(shape):
    return pl.BlockSpec(shape, lambda *_: (0,) * len(shape), pipeline_mode=pl.Buffered(1))


def _params(n_grid_dims):
    return pltpu.CompilerParams(
        dimension_semantics=("arbitrary",) * n_grid_dims,
        vmem_limit_bytes=V7X_VMEM_LIMIT_BYTES)


def _load_weight_bf16(src_hbm, dst_ref, stage_ref, sem):
    n_rows = stage_ref.shape[1]
    n_chunks = src_hbm.shape[0] // n_rows
    assert n_chunks * n_rows == src_hbm.shape[0]

    def chunk_copy(c):
        return pltpu.make_async_copy(src_hbm.at[pl.ds(c * n_rows, n_rows), :],
                                     stage_ref.at[c % 2], sem.at[c % 2])

    chunk_copy(0).start()
    for c in range(n_chunks):
        if c + 1 < n_chunks:
            chunk_copy(c + 1).start()
        chunk_copy(c).wait()
        dst_ref[c * n_rows:(c + 1) * n_rows, :] = stage_ref[c % 2].astype(jnp.bfloat16)


def _convert_specs(convert, n_steps, step_index):
    in_specs, out_specs, out_shapes = [], [], []
    for stack, layer in convert:
        _, rows, cols = stack.shape
        blk = rows // n_steps
        assert blk * n_steps == rows and blk % V7X_BF16_SUBLANES == 0
        in_specs.append(pl.BlockSpec(
            (None, blk, cols), lambda *ids, layer=layer: (layer, step_index(*ids), 0)))
        out_specs.append(pl.BlockSpec((blk, cols), lambda *ids: (step_index(*ids), 0)))
        out_shapes.append(jax.ShapeDtypeStruct((rows, cols), jnp.bfloat16))
    return in_specs, out_specs, out_shapes


def _ffn_ln_body(layer, n_conv, *refs):
    x_ref, wg_in, wu_in, wd_in, gain_ref, bias_ref = refs[:6]
    conv_in = refs[6:6 + n_conv]
    o_ref = refs[6 + n_conv]
    conv_out = refs[7 + n_conv:7 + 2 * n_conv]
    scratch = refs[7 + 2 * n_conv:]
    if layer is None:
        wg_ref, wu_ref, wd_ref = wg_in, wu_in, wd_in
    else:
        wg_ref, wu_ref, wd_ref, stage_in_ref, stage_down_ref, sem = scratch

        @pl.when(pl.program_id(0) == 0)
        def _():
            _load_weight_bf16(wg_in.at[layer], wg_ref, stage_in_ref, sem)
            _load_weight_bf16(wu_in.at[layer], wu_ref, stage_in_ref, sem)
            _load_weight_bf16(wd_in.at[layer], wd_ref, stage_down_ref, sem)

    for src, dst in zip(conv_in, conv_out):
        dst[...] = src[...].astype(jnp.bfloat16)

    for r0 in range(0, x_ref.shape[0], FFN_SUB_TILE):
        rows = slice(r0, r0 + FFN_SUB_TILE)
        x = x_ref[rows, :]
        xb = x.astype(jnp.bfloat16)
        h = None
        for lo, hi in FFN_CHUNKS:
            g = _dot(xb, wg_ref[:, lo:hi])
            u = _dot(xb, wu_ref[:, lo:hi])
            hg = 0.5 * g
            a = ((hg + hg * jnp.tanh(hg)) * u).astype(jnp.bfloat16)
            part = _dot(a, wd_ref[lo:hi, :])
            h = part if h is None else h + part
        y = ALPHA * x + MACARON_WEIGHT * h
        o_ref[rows, :] = _layer_norm(y, gain_ref[...], bias_ref[...])


def _ffn_ln(x, w_gate, w_up, w_down, gain, bias, layer=None, convert=()):
    m = x.shape[0]
    tm = FFN_ROW_TILE
    n_steps = m // tm
    row_spec = pl.BlockSpec((tm, D_MODEL), lambda i: (i, 0))
    if layer is None:
        weight_specs = [_const_spec((D_MODEL, D_FF)), _const_spec((D_MODEL, D_FF)),
                        _const_spec((D_FF, D_MODEL))]
        scratch = []
    else:
        weight_specs = [pl.BlockSpec(memory_space=pl.ANY)] * 3
        scratch = [pltpu.VMEM((D_MODEL, D_FF), jnp.bfloat16),
                   pltpu.VMEM((D_MODEL, D_FF), jnp.bfloat16),
                   pltpu.VMEM((D_FF, D_MODEL), jnp.bfloat16),
                   pltpu.VMEM((2, D_MODEL // WEIGHT_LOAD_CHUNKS, D_FF), jnp.float32),
                   pltpu.VMEM((2, D_FF // WEIGHT_LOAD_CHUNKS, D_MODEL), jnp.float32),
                   pltpu.SemaphoreType.DMA((2,))]
    conv_in, conv_out, conv_shapes = _convert_specs(convert, n_steps, lambda i: i)
    outs = pl.pallas_call(
        functools.partial(_ffn_ln_body, layer, len(convert)),
        grid=(n_steps,),
        in_specs=[row_spec] + weight_specs
                 + [_const_spec((1, D_MODEL)), _const_spec((1, D_MODEL))] + conv_in,
        out_specs=[row_spec] + conv_out,
        out_shape=[jax.ShapeDtypeStruct((m, D_MODEL), jnp.float32)] + conv_shapes,
        scratch_shapes=scratch,
        compiler_params=_params(1),
        name="ffn_ln",
    )(x, w_gate, w_up, w_down, gain, bias, *[stack for stack, _ in convert])
    return outs[0], outs[1:]


def _pool_ln_body(seq_len, xm_ref, xp_ref, xn_ref, win_ref, wgrp_ref, scale_ref, wout_ref,
                  gain_ref, bias_ref, conv_in_ref, o_ref, conv_out_ref, xe_ref, u_ref, lvl_ref):
    conv_out_ref[...] = conv_in_ref[...].astype(jnp.bfloat16)
    ts = xm_ref.shape[0]
    halo = POOL_HALO
    s = pl.program_id(1)
    n_s = pl.num_programs(1)
    xe_ref[0:halo, :] = xp_ref[...]
    xe_ref[halo:halo + ts, :] = xm_ref[...]
    xe_ref[halo + ts:, :] = xn_ref[...]
    tb = POOL_SUB_TILE
    n_ext = tb + 2 * halo
    n_sub = ts // tb
    for k in range(n_sub):
        r0 = k * tb
        u_ext = _dot(xe_ref[r0:r0 + n_ext, :].astype(jnp.bfloat16), win_ref[...])
        row = lax.broadcasted_iota(jnp.int32, (n_ext, 1), 0)
        if k == 0:
            u_ext = jnp.where((row < halo) & (s == 0), 0.0, u_ext)
        if k == n_sub - 1:
            u_ext = jnp.where((row >= halo + tb) & (s == n_s - 1), 0.0, u_ext)
        u_ref[k, 0:n_ext, :] = u_ext
        u_ref[k, n_ext:, :] = jnp.zeros((V7X_SUBLANES, D_MODEL), jnp.float32)

        t_abs = s * ts + r0 + lax.broadcasted_iota(jnp.int32, (tb, 1), 0)
        ys = []
        lvl = 0
        for g, w in enumerate(POOL_WINDOWS):
            half = w // 2
            cols = slice(g * POOL_GROUP_DIM, (g + 1) * POOL_GROUP_DIM)
            src = functools.partial(lambda k_, c_, lo, n: u_ref[k_, lo:lo + n, c_], k, cols)
            width = 1
            while 2 * width < w:
                lvl_ref[k, lvl, 0:n_ext, :] = src(0, n_ext) + src(width, n_ext)
                lvl_ref[k, lvl, n_ext:, :] = jnp.zeros((V7X_SUBLANES, POOL_GROUP_DIM), jnp.float32)
                src = functools.partial(lambda k_, l_, lo, n: lvl_ref[k_, l_, lo:lo + n, :], k, lvl)
                width, lvl = 2 * width, lvl + 1
            start = halo - half
            win_sum = src(start, tb) + src(start + width, tb)
            count = jnp.minimum(t_abs + half, seq_len) - jnp.maximum(t_abs - half, 0)
            mean = win_sum / count.astype(jnp.float32)
            mixed = (mean - u_ref[k, halo:halo + tb, cols]).astype(jnp.bfloat16)
            ys.append(_dot(mixed, wgrp_ref[g]))
        y = (jnp.concatenate(ys, axis=1) * scale_ref[...]).astype(jnp.bfloat16)
        m = _dot(y, wout_ref[...])
        o_ref[r0:r0 + tb, :] = _layer_norm(ALPHA * xm_ref[r0:r0 + tb, :] + m,
                                           gain_ref[...], bias_ref[...])


def _pool_ln(x, w_in, w_group, scale, w_out, gain, bias, convert):
    b, s_len, _ = x.shape
    ts = POOL_ROW_TILE
    halo = POOL_HALO
    blocks_per_tile = ts // halo
    n_halo_blocks = s_len // halo
    n_levels = sum(int(np.log2(w)) - 1 for w in POOL_WINDOWS)
    tb = POOL_SUB_TILE
    buf_rows = tb + 2 * halo + V7X_SUBLANES
    main_spec = pl.BlockSpec((None, ts, D_MODEL), lambda bi, si: (bi, si, 0))
    prev_spec = pl.BlockSpec(
        (None, halo, D_MODEL),
        lambda bi, si: (bi, jnp.maximum(si * blocks_per_tile - 1, 0), 0))
    next_spec = pl.BlockSpec(
        (None, halo, D_MODEL),
        lambda bi, si: (bi, jnp.minimum((si + 1) * blocks_per_tile, n_halo_blocks - 1), 0))
    n_s = s_len // ts
    conv_in, conv_out, conv_shapes = _convert_specs([convert], b * n_s, lambda bi, si: bi * n_s + si)
    outs = pl.pallas_call(
        functools.partial(_pool_ln_body, s_len),
        grid=(b, n_s),
        in_specs=[main_spec, prev_spec, next_spec,
                  _const_spec((D_MODEL, D_MODEL)),
                  _const_spec((N_POOL_GROUPS, POOL_GROUP_DIM, POOL_GROUP_DIM)),
                  _const_spec((1, D_MODEL)),
                  _const_spec((D_MODEL, D_MODEL)),
                  _const_spec((1, D_MODEL)), _const_spec((1, D_MODEL))] + conv_in,
        out_specs=[main_spec] + conv_out,
        out_shape=[jax.ShapeDtypeStruct(x.shape, jnp.float32)] + conv_shapes,
        scratch_shapes=[pltpu.VMEM((ts + 2 * halo, D_MODEL), jnp.float32),
                        pltpu.VMEM((ts // tb, buf_rows, D_MODEL), jnp.float32),
                        pltpu.VMEM((ts // tb, n_levels, buf_rows, POOL_GROUP_DIM), jnp.float32)],
        compiler_params=_params(2),
        name="pool_ln",
    )(x, x, x, w_in, w_group, scale, w_out, gain, bias, convert[0])
    return outs[0], outs[1:]


def _qkv_body(x_ref, w_ref, o0_ref, o1_ref, o2_ref, x_slabs):
    tm = x_ref.shape[0]
    n_slabs = D_MODEL // V7X_LANES
    for c in range(n_slabs):
        x_slabs[c] = x_ref[:, c * V7X_LANES:(c + 1) * V7X_LANES]
    ts = QKV_SUB_TILE
    for r0 in range(0, tm, ts):
        for g, ((_, d), o_ref) in enumerate(zip(DIL_CONFIGS, (o0_ref, o1_ref, o2_ref))):
            n = ts // d
            if d == 1:
                xp = x_ref[r0:r0 + ts, :]
            else:
                xp = jnp.concatenate(
                    [jnp.concatenate([x_slabs[c, pl.ds(r0 + r, n, stride=d), :] for r in range(d)],
                                     axis=0)
                     for c in range(n_slabs)], axis=1)
            xp = xp.astype(jnp.bfloat16)
            for j in range(3):
                lo = (g * 3 + j) * ATTN_WIDTH
                res = _dot(xp, w_ref[:, lo:lo + ATTN_WIDTH])
                if j == 0:
                    res = res * (HEAD_DIM ** -0.5 * LOG2_E)
                res = res.astype(jnp.bfloat16)
                for r in range(d):
                    o_ref[j, r, r0 // d:r0 // d + n, :] = res[r * n:(r + 1) * n]


def _qkv_proj(x, w_qkv, batch, seq_len):
    tm = QKV_ROW_TILE
    tiles_per_seq = seq_len // tm
    return pl.pallas_call(
        _qkv_body,
        grid=(batch * tiles_per_seq,),
        in_specs=[pl.BlockSpec((tm, D_MODEL), lambda i: (i, 0)),
                  _const_spec((D_MODEL, QKV_WIDTH))],
        out_specs=[pl.BlockSpec((3, None, d, tm // d, ATTN_WIDTH),
                                lambda i: (0, i // tiles_per_seq, 0, i % tiles_per_seq, 0))
                   for _, d in DIL_CONFIGS],
        out_shape=[jax.ShapeDtypeStruct((3, batch, d, seq_len // d, ATTN_WIDTH), jnp.bfloat16)
                   for _, d in DIL_CONFIGS],
        scratch_shapes=[pltpu.VMEM((D_MODEL // V7X_LANES, tm, V7X_LANES), jnp.float32)],
        compiler_params=_params(1),
        name="qkv_proj",
    )(x, w_qkv)


def _attn_bias(group):
    window, d = DIL_CONFIGS[group]
    qb, kb = ATTN_Q_BLOCK, ATTN_K_BLOCK
    r = (kb - qb) // 2
    a = np.arange(qb)[:, None]
    c = np.arange(kb)[None, :]
    rel = c - r - a
    band = np.abs(rel) <= r
    valid = np.stack([band, band & (c >= r), band & (c < qb + r)])
    neg_dist = np.where(valid, -(d * np.abs(rel)).astype(np.float64), MASK_VALUE).astype(np.float32)
    slopes = (_alibi_slopes()[group] * LOG2_E).astype(np.float32)
    return jnp.asarray(slopes[None, :, None, None] * neg_dist[:, None])


def _attn_body(n_blocks_total, q_ref, k_ref, kp_ref, kn_ref, v_ref, vp_ref, vn_ref, bias_ref,
               o_ref, stat_ref, ke_ref, ve_ref, s_ref, p_ref):
    n_res, tq, _ = q_ref.shape
    qb, kb = ATTN_Q_BLOCK, ATTN_K_BLOCK
    r = (kb - qb) // 2
    hq = qb // 2
    n_blk = tq // qb
    ti = pl.program_id(2)
    ke_ref[:, 0:r, :] = kp_ref[...]
    ke_ref[:, r:r + tq, :] = k_ref[...]
    ke_ref[:, r + tq:, :] = kn_ref[...]
    for pair in range(N_HEADS // 2):
        src = slice(pair * V7X_LANES, (pair + 1) * V7X_LANES)
        dst = slice(2 * pair * V7X_LANES, (2 * pair + 1) * V7X_LANES)
        one = slice((2 * pair + 1) * V7X_LANES, (2 * pair + 2) * V7X_LANES)
        ve_ref[:, 0:r, dst] = vp_ref[:, :, src]
        ve_ref[:, r:r + tq, dst] = v_ref[:, :, src]
        ve_ref[:, r + tq:, dst] = vn_ref[:, :, src]
        ve_ref[:, :, one] = jnp.ones((n_res, tq + 2 * r, V7X_LANES), jnp.bfloat16)

    low_half = lax.broadcasted_iota(jnp.int32, (qb, V7X_LANES), 1) < HEAD_DIM
    lane = lax.broadcasted_iota(jnp.int32, (hq, STAT_WIDTH), 1)

    for res, blk in [(a, b) for a in range(n_res) for b in range(n_blk)]:
        row0 = blk * qb
        g_blk = ti * n_blk + blk
        variant = jnp.where(g_blk == 0, 1, 0) + jnp.where(g_blk == n_blocks_total - 1, 2, 0)
        stats = [jnp.zeros((hq, STAT_WIDTH), jnp.float32) for _ in range(2)]
        for pair in range(N_HEADS // 2):
            cols = slice(pair * V7X_LANES, (pair + 1) * V7X_LANES)
            buf = pair % ATTN_SCORE_BUFFERS
            q2 = q_ref[res, pl.ds(row0, qb), cols]
            zero = jnp.zeros_like(q2)
            q_stack = jnp.concatenate(
                [jnp.where(low_half, q2, zero), jnp.where(low_half, zero, q2)], axis=0)
            s_ref[buf] = lax.dot_general(q_stack, ke_ref[res, pl.ds(row0, kb), cols],
                                         (((1,), (1,)), ((), ())),
                                         preferred_element_type=jnp.float32)
            for c in range(4):
                head, half = 2 * pair + c // 2, c % 2
                rows = slice(c * hq, (c + 1) * hq)
                sc = s_ref[buf, rows, :] + bias_ref[variant, head, half * hq:(half + 1) * hq, :]
                m = jnp.max(sc, axis=1, keepdims=True)
                p_ref[buf, rows, :] = jnp.exp2(sc - m).astype(jnp.bfloat16)
                stats[half] = jnp.where(lane == head, m, stats[half])
            v_cols = slice(2 * pair * V7X_LANES, (2 * pair + 2) * V7X_LANES)
            o2 = _dot(p_ref[buf], ve_ref[res, pl.ds(row0, kb), v_cols])
            o_ref[res, pl.ds(row0, qb), cols] = jnp.where(
                low_half, o2[:qb, :V7X_LANES], o2[qb:, :V7X_LANES]).astype(o_ref.dtype)
            for c in range(4):
                head, half = 2 * pair + c // 2, c % 2
                stats[half] = jnp.where(lane == N_HEADS + head,
                                        o2[c * hq:(c + 1) * hq, V7X_LANES:], stats[half])
        for half in range(2):
            stat_ref[res, pl.ds(row0 + half * hq, hq), :] = stats[half]


def _attn_group(qkv_g, group):
    window, d = DIL_CONFIGS[group]
    _, batch, _, sub_len, _ = qkv_g.shape
    qb, kb = ATTN_Q_BLOCK, ATTN_K_BLOCK
    r = window // (2 * d)
    assert r == (kb - qb) // 2 and sub_len >= 2 * qb
    tq = min(ATTN_Q_TILE, sub_len)
    n_tiles = sub_len // tq
    n_res = min(d, ATTN_Q_TILE // tq)
    halo_per_tile = tq // r
    n_halo_blocks = sub_len // r

    def main_spec(j):
        return pl.BlockSpec((None, None, n_res, tq, ATTN_WIDTH),
                            lambda bi, ri, ti: (j, bi, ri, ti, 0))

    def prev_spec(j):
        return pl.BlockSpec(
            (None, None, n_res, r, ATTN_WIDTH),
            lambda bi, ri, ti: (j, bi, ri, jnp.maximum(ti * halo_per_tile - 1, 0), 0))

    def next_spec(j):
        return pl.BlockSpec(
            (None, None, n_res, r, ATTN_WIDTH),
            lambda bi, ri, ti: (j, bi, ri, jnp.minimum((ti + 1) * halo_per_tile, n_halo_blocks - 1),
                                0))

    def out_spec(width):
        return pl.BlockSpec((None, n_res, tq, width), lambda bi, ri, ti: (bi, ri, ti, 0))

    return pl.pallas_call(
        functools.partial(_attn_body, sub_len // qb),
        grid=(batch, d // n_res, n_tiles),
        in_specs=[main_spec(0),
                  main_spec(1), prev_spec(1), next_spec(1),
                  main_spec(2), prev_spec(2), next_spec(2),
                  _const_spec((3, N_HEADS, qb, kb))],
        out_specs=[out_spec(ATTN_WIDTH), out_spec(STAT_WIDTH)],
        out_shape=[jax.ShapeDtypeStruct((batch, d, sub_len, ATTN_WIDTH), jnp.bfloat16),
                   jax.ShapeDtypeStruct((batch, d, sub_len, STAT_WIDTH), jnp.float32)],
        scratch_shapes=[pltpu.VMEM((n_res, tq + 2 * r, ATTN_WIDTH), jnp.bfloat16),
                        pltpu.VMEM((n_res, tq + 2 * r, 2 * ATTN_WIDTH), jnp.bfloat16),
                        pltpu.VMEM((ATTN_SCORE_BUFFERS, 2 * qb, kb), jnp.float32),
                        pltpu.VMEM((ATTN_SCORE_BUFFERS, 2 * qb, kb), jnp.bfloat16)],
        compiler_params=_params(3),
        name=f"attn_group{group}",
    )(*([qkv_g] * 7), _attn_bias(group))


def _head_expand_matrix():
    k = np.arange(2 * STAT_WIDTH)[:, None] % STAT_WIDTH
    c = np.arange(ATTN_WIDTH)[None, :] // HEAD_DIM
    return jnp.asarray(k == c, dtype=jnp.bfloat16)


def _attn_out_ln_body(x_ref, o0_ref, o1_ref, o2_ref, l0_ref, l1_ref, l2_ref, e_ref, w_ref,
                      gain_ref, bias_ref, out_ref, o_scr, l_scr):
    o_refs = (o0_ref, o1_ref, o2_ref)
    l_refs = (l0_ref, l1_ref, l2_ref)

    def token_order(ref, scr, g, r0, ts):
        assert DIL_CONFIGS[0][1] == 1 and all(dil > 1 for _, dil in DIL_CONFIGS[1:])
        d, _, width = ref.shape
        if d == 1:
            return ref[0, r0:r0 + ts, :].astype(jnp.float32)
        n_slabs = width // V7X_LANES
        for r in range(d):
            piece = ref[r, r0 // d:(r0 + ts) // d, :].astype(jnp.float32)
            for c in range(n_slabs):
                scr[g - 1, c, pl.ds(r0 + r, ts // d, stride=d), :] = (
                    piece[:, c * V7X_LANES:(c + 1) * V7X_LANES])
        return jnp.concatenate([scr[g - 1, c, r0:r0 + ts, :] for c in range(n_slabs)], axis=1)

    ts = OUT_SUB_TILE
    is_head = lax.broadcasted_iota(jnp.int32, (ts, STAT_WIDTH), 1) < N_HEADS
    for r0 in range(0, x_ref.shape[0], ts):
        ms = [token_order(l_refs[g], l_scr, g, r0, ts) for g in range(N_DIL_GROUPS)]
        ls = [pltpu.roll(m, STAT_WIDTH - N_HEADS, 1) for m in ms]
        top = jnp.maximum(jnp.maximum(ms[0], ms[1]), ms[2])
        es = [jnp.exp2(m - top) for m in ms]
        inv = 1.0 / (es[0] * ls[0] + es[1] * ls[1] + es[2] * ls[2])
        acc = None
        for g in range(N_DIL_GROUPS):
            wt = jnp.where(is_head, es[g] * inv, 0.0)
            hi = wt.astype(jnp.bfloat16)
            lo = (wt - hi.astype(jnp.float32)).astype(jnp.bfloat16)
            w_full = _dot(jnp.concatenate([hi, lo], axis=1), e_ref[...])
            term = w_full * token_order(o_refs[g], o_scr, g, r0, ts)
            acc = term if acc is None else acc + term
        m = _dot(acc.astype(jnp.bfloat16), w_ref[...])
        out_ref[r0:r0 + ts, :] = _layer_norm(ALPHA * x_ref[r0:r0 + ts, :] + m,
                                             gain_ref[...], bias_ref[...])


def _attn_out_ln(x, os_, stats, w_out, gain, bias, seq_len):
    m = x.shape[0]
    tm = OUT_ROW_TILE
    tiles_per_seq = seq_len // tm
    row_spec = pl.BlockSpec((tm, D_MODEL), lambda i: (i, 0))

    def group_spec(d, width):
        return pl.BlockSpec((None, d, tm // d, width),
                            lambda i: (i // tiles_per_seq, 0, i % tiles_per_seq, 0))

    return pl.pallas_call(
        _attn_out_ln_body,
        grid=(m // tm,),
        in_specs=[row_spec]
                 + [group_spec(d, ATTN_WIDTH) for _, d in DIL_CONFIGS]
                 + [group_spec(d, STAT_WIDTH) for _, d in DIL_CONFIGS]
                 + [_const_spec((2 * STAT_WIDTH, ATTN_WIDTH)),
                    _const_spec((ATTN_WIDTH, D_MODEL)),
                    _const_spec((1, D_MODEL)), _const_spec((1, D_MODEL))],
        out_specs=row_spec,
        out_shape=jax.ShapeDtypeStruct((m, D_MODEL), jnp.float32),
        scratch_shapes=[
            pltpu.VMEM((N_DIL_GROUPS - 1, ATTN_WIDTH // V7X_LANES, tm, V7X_LANES), jnp.float32),
            pltpu.VMEM((N_DIL_GROUPS - 1, STAT_WIDTH // V7X_LANES, tm, V7X_LANES), jnp.float32)],
        compiler_params=_params(1),
        name="attn_out_ln",
    )(x, *os_, *stats, _head_expand_matrix(), w_out, gain, bias)


def kernel(x, ffn1_w_gate, ffn1_w_up, ffn1_w_down, ffn2_w_gate, ffn2_w_up, ffn2_w_down,
           ln_gain, ln_bias, pool_w_in, pool_w_group, pool_scale, pool_w_out,
           attn_w_qkv, attn_w_out):
    batch, seq_len, _ = x.shape
    bf16 = jnp.bfloat16
    n_mixers = 2
    h = x.reshape(batch * seq_len, D_MODEL)

    def ln_params(i, j):
        return ln_gain[i, j].reshape(1, D_MODEL), ln_bias[i, j].reshape(1, D_MODEL)

    ffn_stacks = [(ffn1_w_gate, ffn1_w_up, ffn1_w_down), (ffn2_w_gate, ffn2_w_up, ffn2_w_down)]
    ffn_order = [(which, i) for i in range(DEPTH) for which in range(2)]
    ffn_ready = [None]

    def ffn(k, h, gain, bias):
        which, layer = ffn_order[k]
        convert = ()
        if k + 1 < len(ffn_order):
            nxt_which, nxt_layer = ffn_order[k + 1]
            convert = [(w, nxt_layer) for w in ffn_stacks[nxt_which]]
        if ffn_ready[0] is None:
            h, ffn_ready[0] = _ffn_ln(h, *ffn_stacks[which], gain, bias, layer=layer, convert=convert)
        else:
            h, ffn_ready[0] = _ffn_ln(h, *ffn_ready[0], gain, bias, convert=convert)
        return h

    w_qkv_ready = {}
    for i in range(DEPTH):
        h = ffn(2 * i, h, *ln_params(i, 0))
        if i % n_mixers == 0:
            p = i // n_mixers
            assert i + 1 < DEPTH
            h, (w_qkv_ready[(i + 1) // n_mixers],) = _pool_ln(
                h.reshape(batch, seq_len, D_MODEL), pool_w_in[p].astype(bf16),
                pool_w_group[p].astype(bf16), pool_scale[p].reshape(1, D_MODEL),
                pool_w_out[p].astype(bf16), *ln_params(i, 1), (attn_w_qkv, (i + 1) // n_mixers))
            h = h.reshape(batch * seq_len, D_MODEL)
        else:
            a = i // n_mixers
            qkv = _qkv_proj(h, w_qkv_ready[a], batch, seq_len)
            outs = [_attn_group(qkv[g], g) for g in range(N_DIL_GROUPS)]
            h = _attn_out_ln(h, [o for o, _ in outs], [l for _, l in outs],
                             attn_w_out[a].astype(bf16), *ln_params(i, 1), seq_len)
        h = ffn(2 * i + 1, h, *ln_params(i, 2))
    return h.reshape(batch, seq_len, D_MODEL)
```

```python
import functools

import numpy as np
import jax
import jax.numpy as jnp
from jax import lax
from jax.experimental import pallas as pl
from jax.experimental.pallas import tpu as pltpu

D_MODEL = 1024
DEPTH = 2
POOL_WINDOWS = (2, 4, 8, 16)
N_POOL_GROUPS = len(POOL_WINDOWS)
POOL_GROUP_DIM = D_MODEL // N_POOL_GROUPS
POOL_HALO = max(POOL_WINDOWS) // 2
HEAD_DIM = 64
N_HEADS = D_MODEL // HEAD_DIM
DIL_CONFIGS = ((128, 1), (512, 4), (2048, 16))
N_DIL_GROUPS = len(DIL_CONFIGS)
ATTN_WIDTH = N_HEADS * HEAD_DIM
QKV_WIDTH = N_DIL_GROUPS * 3 * ATTN_WIDTH
D_FF = 2816
MACARON_WEIGHT = 0.5
ALPHA = (2.0 * DEPTH) ** 0.25
LN_EPS = 1e-5
MASK_VALUE = -1e30
LOG2_E = float(np.log2(np.e))

V7X_LANES = 128
V7X_MXU_DIM = 256
V7X_SUBLANES = 8
V7X_BF16_SUBLANES = 16
V7X_VMEM_LIMIT_BYTES = 56 * 1024 * 1024

FFN_ROW_TILE = 1024
FFN_SUB_TILE = 256
FFN_CHUNKS = ((0, 6 * V7X_MXU_DIM), (6 * V7X_MXU_DIM, D_FF))
WEIGHT_LOAD_CHUNKS = 8
POOL_ROW_TILE = 1024
POOL_SUB_TILE = 256
QKV_ROW_TILE = 512
QKV_SUB_TILE = 512
ATTN_Q_TILE = 1024
ATTN_Q_BLOCK = 128
ATTN_K_BLOCK = 256
ATTN_SCORE_BUFFERS = 2
OUT_ROW_TILE = 1024
OUT_SUB_TILE = 512
STAT_WIDTH = V7X_LANES


def _alibi_slopes():
    n = N_DIL_GROUPS * N_HEADS
    s = 2.0 ** (-8.0 * np.arange(1, n + 1) / n)
    return s.reshape(N_DIL_GROUPS, N_HEADS)


def _layer_norm(y, gain, bias):
    mu = jnp.mean(y, axis=-1, keepdims=True)
    yc = y - mu
    var = jnp.mean(yc * yc, axis=-1, keepdims=True)
    return yc * lax.rsqrt(var + LN_EPS) * gain + bias


def _dot(a, b):
    return jnp.dot(a, b, preferred_element_type=jnp.float32)


def _const_spec(shape):
    return pl.BlockSpec(shape, lambda *_: (0,) * len(shape), pipeline_mode=pl.Buffered(1))


def _params(n_grid_dims):
    return pltpu.CompilerParams(
        dimension_semantics=("arbitrary",) * n_grid_dims,
        vmem_limit_bytes=V7X_VMEM_LIMIT_BYTES)


def _load_weight_bf16(src_hbm, dst_ref, stage_ref, sem):
    n_rows = stage_ref.shape[1]
    n_chunks = src_hbm.shape[0] // n_rows
    assert n_chunks * n_rows == src_hbm.shape[0]

    def chunk_copy(c):
        return pltpu.make_async_copy(src_hbm.at[pl.ds(c * n_rows, n_rows), :],
                                     stage_ref.at[c % 2], sem.at[c % 2])

    chunk_copy(0).start()
    for c in range(n_chunks):
        if c + 1 < n_chunks:
            chunk_copy(c + 1).start()
        chunk_copy(c).wait()
        dst_ref[c * n_rows:(c + 1) * n_rows, :] = stage_ref[c % 2].astype(jnp.bfloat16)


def _convert_specs(convert, n_steps, step_index):
    in_specs, out_specs, out_shapes = [], [], []
    for stack, layer in convert:
        _, rows, cols = stack.shape
        blk = rows // n_steps
        assert blk * n_steps == rows and blk % V7X_BF16_SUBLANES == 0
        in_specs.append(pl.BlockSpec(
            (None, blk, cols), lambda *ids, layer=layer: (layer, step_index(*ids), 0)))
        out_specs.append(pl.BlockSpec((blk, cols), lambda *ids: (step_index(*ids), 0)))
        out_shapes.append(jax.ShapeDtypeStruct((rows, cols), jnp.bfloat16))
    return in_specs, out_specs, out_shapes


def _ffn_ln_body(layer, n_conv, *refs):
    x_ref, wg_in, wu_in, wd_in, gain_ref, bias_ref = refs[:6]
    conv_in = refs[6:6 + n_conv]
    o_ref = refs[6 + n_conv]
    conv_out = refs[7 + n_conv:7 + 2 * n_conv]
    scratch = refs[7 + 2 * n_conv:]
    if layer is None:
        wg_ref, wu_ref, wd_ref = wg_in, wu_in, wd_in
    else:
        wg_ref, wu_ref, wd_ref, stage_in_ref, stage_down_ref, sem = scratch

        @pl.when(pl.program_id(0) == 0)
        def _():
            _load_weight_bf16(wg_in.at[layer], wg_ref, stage_in_ref, sem)
            _load_weight_bf16(wu_in.at[layer], wu_ref, stage_in_ref, sem)
            _load_weight_bf16(wd_in.at[layer], wd_ref, stage_down_ref, sem)

    for src, dst in zip(conv_in, conv_out):
        dst[...] = src[...].astype(jnp.bfloat16)

    for r0 in range(0, x_ref.shape[0], FFN_SUB_TILE):
        rows = slice(r0, r0 + FFN_SUB_TILE)
        x = x_ref[rows, :]
        xb = x.astype(jnp.bfloat16)
        h = None
        for lo, hi in FFN_CHUNKS:
            g = _dot(xb, wg_ref[:, lo:hi])
            u = _dot(xb, wu_ref[:, lo:hi])
            hg = 0.5 * g
            a = ((hg + hg * jnp.tanh(hg)) * u).astype(jnp.bfloat16)
            part = _dot(a, wd_ref[lo:hi, :])
            h = part if h is None else h + part
        y = ALPHA * x + MACARON_WEIGHT * h
        o_ref[rows, :] = _layer_norm(y, gain_ref[...], bias_ref[...])


def _ffn_ln(x, w_gate, w_up, w_down, gain, bias, layer=None, convert=()):
    m = x.shape[0]
    tm = FFN_ROW_TILE
    n_steps = m // tm
    row_spec = pl.BlockSpec((tm, D_MODEL), lambda i: (i, 0))
    if layer is None:
        weight_specs = [_const_spec((D_MODEL, D_FF)), _const_spec((D_MODEL, D_FF)),
                        _const_spec((D_FF, D_MODEL))]
        scratch = []
    else:
        weight_specs = [pl.BlockSpec(memory_space=pl.ANY)] * 3
        scratch = [pltpu.VMEM((D_MODEL, D_FF), jnp.bfloat16),
                   pltpu.VMEM((D_MODEL, D_FF), jnp.bfloat16),
                   pltpu.VMEM((D_FF, D_MODEL), jnp.bfloat16),
                   pltpu.VMEM((2, D_MODEL // WEIGHT_LOAD_CHUNKS, D_FF), jnp.float32),
                   pltpu.VMEM((2, D_FF // WEIGHT_LOAD_CHUNKS, D_MODEL), jnp.float32),
                   pltpu.SemaphoreType.DMA((2,))]
    conv_in, conv_out, conv_shapes = _convert_specs(convert, n_steps, lambda i: i)
    outs = pl.pallas_call(
        functools.partial(_ffn_ln_body, layer, len(convert)),
        grid=(n_steps,),
        in_specs=[row_spec] + weight_specs
                 + [_const_spec((1, D_MODEL)), _const_spec((1, D_MODEL))] + conv_in,
        out_specs=[row_spec] + conv_out,
        out_shape=[jax.ShapeDtypeStruct((m, D_MODEL), jnp.float32)] + conv_shapes,
        scratch_shapes=scratch,
        compiler_params=_params(1),
        name="ffn_ln",
    )(x, w_gate, w_up, w_down, gain, bias, *[stack for stack, _ in convert])
    return outs[0], outs[1:]


def _pool_ln_body(seq_len, xm_ref, xp_ref, xn_ref, win_ref, wgrp_ref, scale_ref, wout_ref,
                  gain_ref, bias_ref, conv_in_ref, o_ref, conv_out_ref, xe_ref, u_ref, lvl_ref):
    conv_out_ref[...] = conv_in_ref[...].astype(jnp.bfloat16)
    ts = xm_ref.shape[0]
    halo = POOL_HALO
    s = pl.program_id(1)
    n_s = pl.num_programs(1)
    xe_ref[0:halo, :] = xp_ref[...]
    xe_ref[halo:halo + ts, :] = xm_ref[...]
    xe_ref[halo + ts:, :] = xn_ref[...]
    tb = POOL_SUB_TILE
    n_ext = tb + 2 * halo
    n_sub = ts // tb
    for k in range(n_sub):
        r0 = k * tb
        u_ext = _dot(xe_ref[r0:r0 + n_ext, :].astype(jnp.bfloat16), win_ref[...])
        row = lax.broadcasted_iota(jnp.int32, (n_ext, 1), 0)
        if k == 0:
            u_ext = jnp.where((row < halo) & (s == 0), 0.0, u_ext)
        if k == n_sub - 1:
            u_ext = jnp.where((row >= halo + tb) & (s == n_s - 1), 0.0, u_ext)
        u_ref[k, 0:n_ext, :] = u_ext
        u_ref[k, n_ext:, :] = jnp.zeros((V7X_SUBLANES, D_MODEL), jnp.float32)

        t_abs = s * ts + r0 + lax.broadcasted_iota(jnp.int32, (tb, 1), 0)
        ys = []
        lvl = 0
        for g, w in enumerate(POOL_WINDOWS):
            half = w // 2
            cols = slice(g * POOL_GROUP_DIM, (g + 1) * POOL_GROUP_DIM)
            src = functools.partial(lambda k_, c_, lo, n: u_ref[k_, lo:lo + n, c_], k, cols)
            width = 1
            while 2 * width < w:
                lvl_ref[k, lvl, 0:n_ext, :] = src(0, n_ext) + src(width, n_ext)
                lvl_ref[k, lvl, n_ext:, :] = jnp.zeros((V7X_SUBLANES, POOL_GROUP_DIM), jnp.float32)
                src = functools.partial(lambda k_, l_, lo, n: lvl_ref[k_, l_, lo:lo + n, :], k, lvl)
                width, lvl = 2 * width, lvl + 1
            start = halo - half
            win_sum = src(start, tb) + src(start + width, tb)
            count = jnp.minimum(t_abs + half, seq_len) - jnp.maximum(t_abs - half, 0)
            mean = win_sum / count.astype(jnp.float32)
            mixed = (mean - u_ref[k, halo:halo + tb, cols]).astype(jnp.bfloat16)
            ys.append(_dot(mixed, wgrp_ref[g]))
        y = (jnp.concatenate(ys, axis=1) * scale_ref[...]).astype(jnp.bfloat16)
        m = _dot(y, wout_ref[...])
        o_ref[r0:r0 + tb, :] = _layer_norm(ALPHA * xm_ref[r0:r0 + tb, :] + m,
                                           gain_ref[...], bias_ref[...])


def _pool_ln(x, w_in, w_group, scale, w_out, gain, bias, convert):
    b, s_len, _ = x.shape
    ts = POOL_ROW_TILE
    halo = POOL_HALO
    blocks_per_tile = ts // halo
    n_halo_blocks = s_len // halo
    n_levels = sum(int(np.log2(w)) - 1 for w in POOL_WINDOWS)
    tb = POOL_SUB_TILE
    buf_rows = tb + 2 * halo + V7X_SUBLANES
    main_spec = pl.BlockSpec((None, ts, D_MODEL), lambda bi, si: (bi, si, 0))
    prev_spec = pl.BlockSpec(
        (None, halo, D_MODEL),
        lambda bi, si: (bi, jnp.maximum(si * blocks_per_tile - 1, 0), 0))
    next_spec = pl.BlockSpec(
        (None, halo, D_MODEL),
        lambda bi, si: (bi, jnp.minimum((si + 1) * blocks_per_tile, n_halo_blocks - 1), 0))
    n_s = s_len // ts
    conv_in, conv_out, conv_shapes = _convert_specs([convert], b * n_s, lambda bi, si: bi * n_s + si)
    outs = pl.pallas_call(
        functools.partial(_pool_ln_body, s_len),
        grid=(b, n_s),
        in_specs=[main_spec, prev_spec, next_spec,
                  _const_spec((D_MODEL, D_MODEL)),
                  _const_spec((N_POOL_GROUPS, POOL_GROUP_DIM, POOL_GROUP_DIM)),
                  _const_spec((1, D_MODEL)),
                  _const_spec((D_MODEL, D_MODEL)),
                  _const_spec((1, D_MODEL)), _const_spec((1, D_MODEL))] + conv_in,
        out_specs=[main_spec] + conv_out,
        out_shape=[jax.ShapeDtypeStruct(x.shape, jnp.float32)] + conv_shapes,
        scratch_shapes=[pltpu.VMEM((ts + 2 * halo, D_MODEL), jnp.float32),
                        pltpu.VMEM((ts // tb, buf_rows, D_MODEL), jnp.float32),
                        pltpu.VMEM((ts // tb, n_levels, buf_rows, POOL_GROUP_DIM), jnp.float32)],
        compiler_params=_params(2),
        name="pool_ln",
    )(x, x, x, w_in, w_group, scale, w_out, gain, bias, convert[0])
    return outs[0], outs[1:]


def _qkv_body(x_ref, w_ref, o0_ref, o1_ref, o2_ref, x_slabs):
    tm = x_ref.shape[0]
    n_slabs = D_MODEL // V7X_LANES
    for c in range(n_slabs):
        x_slabs[c] = x_ref[:, c * V7X_LANES:(c + 1) * V7X_LANES]
    ts = QKV_SUB_TILE
    for r0 in range(0, tm, ts):
        for g, ((_, d), o_ref) in enumerate(zip(DIL_CONFIGS, (o0_ref, o1_ref, o2_ref))):
            n = ts // d
            if d == 1:
                xp = x_ref[r0:r0 + ts, :]
            else:
                xp = jnp.concatenate(
                    [jnp.concatenate([x_slabs[c, pl.ds(r0 + r, n, stride=d), :] for r in range(d)],
                                     axis=0)
                     for c in range(n_slabs)], axis=1)
            xp = xp.astype(jnp.bfloat16)
            for j in range(3):
                lo = (g * 3 + j) * ATTN_WIDTH
                res = _dot(xp, w_ref[:, lo:lo + ATTN_WIDTH])
                if j == 0:
                    res = res * (HEAD_DIM ** -0.5 * LOG2_E)
                res = res.astype(jnp.bfloat16)
                for r in range(d):
                    o_ref[j, r, r0 // d:r0 // d + n, :] = res[r * n:(r + 1) * n]


def _qkv_proj(x, w_qkv, batch, seq_len):
    tm = QKV_ROW_TILE
    tiles_per_seq = seq_len // tm
    return pl.pallas_call(
        _qkv_body,
        grid=(batch * tiles_per_seq,),
        in_specs=[pl.BlockSpec((tm, D_MODEL), lambda i: (i, 0)),
                  _const_spec((D_MODEL, QKV_WIDTH))],
        out_specs=[pl.BlockSpec((3, None, d, tm // d, ATTN_WIDTH),
                                lambda i: (0, i // tiles_per_seq, 0, i % tiles_per_seq, 0))
                   for _, d in DIL_CONFIGS],
        out_shape=[jax.ShapeDtypeStruct((3, batch, d, seq_len // d, ATTN_WIDTH), jnp.bfloat16)
                   for _, d in DIL_CONFIGS],
        scratch_shapes=[pltpu.VMEM((D_MODEL // V7X_LANES, tm, V7X_LANES), jnp.float32)],
        compiler_params=_params(1),
        name="qkv_proj",
    )(x, w_qkv)


def _attn_bias(group):
    window, d = DIL_CONFIGS[group]
    qb, kb = ATTN_Q_BLOCK, ATTN_K_BLOCK
    r = (kb - qb) // 2
    a = np.arange(qb)[:, None]
    c = np.arange(kb)[None, :]
    rel = c - r - a
    band = np.abs(rel) <= r
    valid = np.stack([band, band & (c >= r), band & (c < qb + r)])
    neg_dist = np.where(valid, -(d * np.abs(rel)).astype(np.float64), MASK_VALUE).astype(np.float32)
    slopes = (_alibi_slopes()[group] * LOG2_E).astype(np.float32)
    return jnp.asarray(slopes[None, :, None, None] * neg_dist[:, None])


def _attn_body(n_blocks_total, q_ref, k_ref, kp_ref, kn_ref, v_ref, vp_ref, vn_ref, bias_ref,
               o_ref, stat_ref, ke_ref, ve_ref, s_ref, p_ref):
    n_res, tq, _ = q_ref.shape
    qb, kb = ATTN_Q_BLOCK, ATTN_K_BLOCK
    r = (kb - qb) // 2
    hq = qb // 2
    n_blk = tq // qb
    ti = pl.program_id(2)
    ke_ref[:, 0:r, :] = kp_ref[...]
    ke_ref[:, r:r + tq, :] = k_ref[...]
    ke_ref[:, r + tq:, :] = kn_ref[...]
    for pair in range(N_HEADS // 2):
        src = slice(pair * V7X_LANES, (pair + 1) * V7X_LANES)
        dst = slice(2 * pair * V7X_LANES, (2 * pair + 1) * V7X_LANES)
        one = slice((2 * pair + 1) * V7X_LANES, (2 * pair + 2) * V7X_LANES)
        ve_ref[:, 0:r, dst] = vp_ref[:, :, src]
        ve_ref[:, r:r + tq, dst] = v_ref[:, :, src]
        ve_ref[:, r + tq:, dst] = vn_ref[:, :, src]
        ve_ref[:, :, one] = jnp.ones((n_res, tq + 2 * r, V7X_LANES), jnp.bfloat16)

    low_half = lax.broadcasted_iota(jnp.int32, (qb, V7X_LANES), 1) < HEAD_DIM

    for res, blk in [(a, b) for a in range(n_res) for b in range(n_blk)]:
        row0 = blk * qb
        g_blk = ti * n_blk + blk
        variant = jnp.where(g_blk == 0, 1, 0) + jnp.where(g_blk == n_blocks_total - 1, 2, 0)
        stat_ref[res, pl.ds(row0, qb), :] = jnp.zeros((qb, STAT_WIDTH), jnp.float32)
        for pair in range(N_HEADS // 2):
            cols = slice(pair * V7X_LANES, (pair + 1) * V7X_LANES)
            buf = pair % ATTN_SCORE_BUFFERS
            q2 = q_ref[res, pl.ds(row0, qb), cols]
            zero = jnp.zeros_like(q2)
            q_stack = jnp.concatenate(
                [jnp.where(low_half, q2, zero), jnp.where(low_half, zero, q2)], axis=0)
            s_ref[buf] = lax.dot_general(q_stack, ke_ref[res, pl.ds(row0, kb), cols],
                                         (((1,), (1,)), ((), ())),
                                         preferred_element_type=jnp.float32)
            for c in range(4):
                head, half = 2 * pair + c // 2, c % 2
                rows = slice(c * hq, (c + 1) * hq)
                sc = s_ref[buf, rows, :] + bias_ref[variant, head, half * hq:(half + 1) * hq, :]
                m = jnp.max(sc, axis=1, keepdims=True)
                p_ref[buf, rows, :] = jnp.exp2(sc - m).astype(jnp.bfloat16)
                stat_ref[res, pl.ds(row0 + half * hq, hq), head:head + 1] = m
            v_cols = slice(2 * pair * V7X_LANES, (2 * pair + 2) * V7X_LANES)
            o2 = _dot(p_ref[buf], ve_ref[res, pl.ds(row0, kb), v_cols])
            o_ref[res, pl.ds(row0, qb), cols] = jnp.where(
                low_half, o2[:qb, :V7X_LANES], o2[qb:, :V7X_LANES]).astype(o_ref.dtype)
            for hh in range(2):
                lane_l = N_HEADS + 2 * pair + hh
                stat_ref[res, pl.ds(row0, qb), lane_l:lane_l + 1] = (
                    o2[hh * qb:(hh + 1) * qb, V7X_LANES + lane_l:V7X_LANES + lane_l + 1])


def _attn_group(qkv_g, group):
    window, d = DIL_CONFIGS[group]
    _, batch, _, sub_len, _ = qkv_g.shape
    qb, kb = ATTN_Q_BLOCK, ATTN_K_BLOCK
    r = window // (2 * d)
    assert r == (kb - qb) // 2 and sub_len >= 2 * qb
    tq = min(ATTN_Q_TILE, sub_len)
    n_tiles = sub_len // tq
    n_res = min(d, ATTN_Q_TILE // tq)
    halo_per_tile = tq // r
    n_halo_blocks = sub_len // r

    def main_spec(j):
        return pl.BlockSpec((None, None, n_res, tq, ATTN_WIDTH),
                            lambda bi, ri, ti: (j, bi, ri, ti, 0))

    def prev_spec(j):
        return pl.BlockSpec(
            (None, None, n_res, r, ATTN_WIDTH),
            lambda bi, ri, ti: (j, bi, ri, jnp.maximum(ti * halo_per_tile - 1, 0), 0))

    def next_spec(j):
        return pl.BlockSpec(
            (None, None, n_res, r, ATTN_WIDTH),
            lambda bi, ri, ti: (j, bi, ri, jnp.minimum((ti + 1) * halo_per_tile, n_halo_blocks - 1),
                                0))

    def out_spec(width):
        return pl.BlockSpec((None, n_res, tq, width), lambda bi, ri, ti: (bi, ri, ti, 0))

    return pl.pallas_call(
        functools.partial(_attn_body, sub_len // qb),
        grid=(batch, d // n_res, n_tiles),
        in_specs=[main_spec(0),
                  main_spec(1), prev_spec(1), next_spec(1),
                  main_spec(2), prev_spec(2), next_spec(2),
                  _const_spec((3, N_HEADS, qb, kb))],
        out_specs=[out_spec(ATTN_WIDTH), out_spec(STAT_WIDTH)],
        out_shape=[jax.ShapeDtypeStruct((batch, d, sub_len, ATTN_WIDTH), jnp.bfloat16),
                   jax.ShapeDtypeStruct((batch, d, sub_len, STAT_WIDTH), jnp.float32)],
        scratch_shapes=[pltpu.VMEM((n_res, tq + 2 * r, ATTN_WIDTH), jnp.bfloat16),
                        pltpu.VMEM((n_res, tq + 2 * r, 2 * ATTN_WIDTH), jnp.bfloat16),
                        pltpu.VMEM((ATTN_SCORE_BUFFERS, 2 * qb, kb), jnp.float32),
                        pltpu.VMEM((ATTN_SCORE_BUFFERS, 2 * qb, kb), jnp.bfloat16)],
        compiler_params=_params(3),
        name=f"attn_group{group}",
    )(*([qkv_g] * 7), _attn_bias(group))


def _head_expand_matrix():
    k = np.arange(2 * STAT_WIDTH)[:, None] % STAT_WIDTH
    c = np.arange(ATTN_WIDTH)[None, :] // HEAD_DIM
    return jnp.asarray(k == c, dtype=jnp.bfloat16)


def _attn_out_ln_body(x_ref, o0_ref, o1_ref, o2_ref, l0_ref, l1_ref, l2_ref, e_ref, w_ref,
                      gain_ref, bias_ref, out_ref, o_scr, l_scr):
    o_refs = (o0_ref, o1_ref, o2_ref)
    l_refs = (l0_ref, l1_ref, l2_ref)

    def token_order(ref, scr, g, r0, ts):
        assert DIL_CONFIGS[0][1] == 1 and all(dil > 1 for _, dil in DIL_CONFIGS[1:])
        d, _, width = ref.shape
        if d == 1:
            return ref[0, r0:r0 + ts, :].astype(jnp.float32)
        n_slabs = width // V7X_LANES
        for r in range(d):
            piece = ref[r, r0 // d:(r0 + ts) // d, :].astype(jnp.float32)
            for c in range(n_slabs):
                scr[g - 1, c, pl.ds(r0 + r, ts // d, stride=d), :] = (
                    piece[:, c * V7X_LANES:(c + 1) * V7X_LANES])
        return jnp.concatenate([scr[g - 1, c, r0:r0 + ts, :] for c in range(n_slabs)], axis=1)

    ts = OUT_SUB_TILE
    is_head = lax.broadcasted_iota(jnp.int32, (ts, STAT_WIDTH), 1) < N_HEADS
    for r0 in range(0, x_ref.shape[0], ts):
        ms = [token_order(l_refs[g], l_scr, g, r0, ts) for g in range(N_DIL_GROUPS)]
        ls = [pltpu.roll(m, STAT_WIDTH - N_HEADS, 1) for m in ms]
        top = jnp.maximum(jnp.maximum(ms[0], ms[1]), ms[2])
        es = [jnp.exp2(m - top) for m in ms]
        inv = 1.0 / (es[0] * ls[0] + es[1] * ls[1] + es[2] * ls[2])
        acc = None
        for g in range(N_DIL_GROUPS):
            wt = jnp.where(is_head, es[g] * inv, 0.0)
            hi = wt.astype(jnp.bfloat16)
            lo = (wt - hi.astype(jnp.float32)).astype(jnp.bfloat16)
            w_full = _dot(jnp.concatenate([hi, lo], axis=1), e_ref[...])
            term = w_full * token_order(o_refs[g], o_scr, g, r0, ts)
            acc = term if acc is None else acc + term
        m = _dot(acc.astype(jnp.bfloat16), w_ref[...])
        out_ref[r0:r0 + ts, :] = _layer_norm(ALPHA * x_ref[r0:r0 + ts, :] + m,
                                             gain_ref[...], bias_ref[...])


def _attn_out_ln(x, os_, stats, w_out, gain, bias, seq_len):
    m = x.shape[0]
    tm = OUT_ROW_TILE
    tiles_per_seq = seq_len // tm
    row_spec = pl.BlockSpec((tm, D_MODEL), lambda i: (i, 0))

    def group_spec(d, width):
        return pl.BlockSpec((None, d, tm // d, width),
                            lambda i: (i // tiles_per_seq, 0, i % tiles_per_seq, 0))

    return pl.pallas_call(
        _attn_out_ln_body,
        grid=(m // tm,),
        in_specs=[row_spec]
                 + [group_spec(d, ATTN_WIDTH) for _, d in DIL_CONFIGS]
                 + [group_spec(d, STAT_WIDTH) for _, d in DIL_CONFIGS]
                 + [_const_spec((2 * STAT_WIDTH, ATTN_WIDTH)),
                    _const_spec((ATTN_WIDTH, D_MODEL)),
                    _const_spec((1, D_MODEL)), _const_spec((1, D_MODEL))],
        out_specs=row_spec,
        out_shape=jax.ShapeDtypeStruct((m, D_MODEL), jnp.float32),
        scratch_shapes=[
            pltpu.VMEM((N_DIL_GROUPS - 1, ATTN_WIDTH // V7X_LANES, tm, V7X_LANES), jnp.float32),
            pltpu.VMEM((N_DIL_GROUPS - 1, STAT_WIDTH // V7X_LANES, tm, V7X_LANES), jnp.float32)],
        compiler_params=_params(1),
        name="attn_out_ln",
    )(x, *os_, *stats, _head_expand_matrix(), w_out, gain, bias)


def kernel(x, ffn1_w_gate, ffn1_w_up, ffn1_w_down, ffn2_w_gate, ffn2_w_up, ffn2_w_down,
           ln_gain, ln_bias, pool_w_in, pool_w_group, pool_scale, pool_w_out,
           attn_w_qkv, attn_w_out):
    batch, seq_len, _ = x.shape
    n_mixers = 2
    h = x.reshape(batch * seq_len, D_MODEL)

    def ln_params(i, j):
        return ln_gain[i, j].reshape(1, D_MODEL), ln_bias[i, j].reshape(1, D_MODEL)

    ffn_stacks = [(ffn1_w_gate, ffn1_w_up, ffn1_w_down), (ffn2_w_gate, ffn2_w_up, ffn2_w_down)]
    ffn_order = [(which, i) for i in range(DEPTH) for which in range(2)]
    ffn_ready = [None]

    def ffn(k, h, gain, bias, extra=()):
        which, layer = ffn_order[k]
        convert = []
        if k + 1 < len(ffn_order):
            nxt_which, nxt_layer = ffn_order[k + 1]
            convert = [(w, nxt_layer) for w in ffn_stacks[nxt_which]]
        n_next = len(convert)
        convert = convert + list(extra)
        if ffn_ready[0] is None:
            h, conv = _ffn_ln(h, *ffn_stacks[which], gain, bias, layer=layer, convert=convert)
        else:
            h, conv = _ffn_ln(h, *ffn_ready[0], gain, bias, convert=convert)
        ffn_ready[0] = conv[:n_next]
        return h, conv[n_next:]

    w_qkv_ready = {}
    for i in range(DEPTH):
        if i % n_mixers == 0:
            p = i // n_mixers
            group_stack = pool_w_group.reshape(-1, D_MODEL, POOL_GROUP_DIM)
            h, (w_in, w_group, w_out) = ffn(2 * i, h, *ln_params(i, 0),
                                            [(pool_w_in, p), (group_stack, p), (pool_w_out, p)])
            assert i + 1 < DEPTH
            h, (w_qkv_ready[(i + 1) // n_mixers],) = _pool_ln(
                h.reshape(batch, seq_len, D_MODEL), w_in,
                w_group.reshape(N_POOL_GROUPS, POOL_GROUP_DIM, POOL_GROUP_DIM),
                pool_scale[p].reshape(1, D_MODEL), w_out, *ln_params(i, 1),
                (attn_w_qkv, (i + 1) // n_mixers))
            h = h.reshape(batch * seq_len, D_MODEL)
        else:
            a = i // n_mixers
            h, (w_out,) = ffn(2 * i, h, *ln_params(i, 0), [(attn_w_out, a)])
            qkv = _qkv_proj(h, w_qkv_ready[a], batch, seq_len)
            outs = [_attn_group(qkv[g], g) for g in range(N_DIL_GROUPS)]
            h = _attn_out_ln(h, [o for o, _ in outs], [l for _, l in outs],
                             w_out, *ln_params(i, 1), seq_len)
        h, _ = ffn(2 * i + 1, h, *ln_params(i, 2))
    return h.reshape(batch, seq_len, D_MODEL)
```

```python
import functools

import numpy as np
import jax
import jax.numpy as jnp
from jax import lax
from jax.experimental import pallas as pl
from jax.experimental.pallas import tpu as pltpu

D_MODEL = 1024
DEPTH = 2
POOL_WINDOWS = (2, 4, 8, 16)
N_POOL_GROUPS = len(POOL_WINDOWS)
POOL_GROUP_DIM = D_MODEL // N_POOL_GROUPS
POOL_HALO = max(POOL_WINDOWS) // 2
HEAD_DIM = 64
N_HEADS = D_MODEL // HEAD_DIM
DIL_CONFIGS = ((128, 1), (512, 4), (2048, 16))
N_DIL_GROUPS = len(DIL_CONFIGS)
ATTN_WIDTH = N_HEADS * HEAD_DIM
QKV_WIDTH = N_DIL_GROUPS * 3 * ATTN_WIDTH
D_FF = 2816
MACARON_WEIGHT = 0.5
ALPHA = (2.0 * DEPTH) ** 0.25
LN_EPS = 1e-5
MASK_VALUE = -1e30
LOG2_E = float(np.log2(np.e))

V7X_LANES = 128
V7X_MXU_DIM = 256
V7X_SUBLANES = 8
V7X_BF16_SUBLANES = 16
V7X_VMEM_LIMIT_BYTES = 56 * 1024 * 1024

FFN_ROW_TILE = 1024
FFN_SUB_TILE = 256
FFN_CHUNKS = ((0, 6 * V7X_MXU_DIM), (6 * V7X_MXU_DIM, D_FF))
WEIGHT_LOAD_CHUNKS = 8
POOL_ROW_TILE = 1024
POOL_SUB_TILE = 256
QKV_ROW_TILE = 512
QKV_SUB_TILE = 512
ATTN_Q_TILE = 1024
ATTN_Q_BLOCK = 128
ATTN_K_BLOCK = 256
ATTN_SCORE_BUFFERS = 2
OUT_ROW_TILE = 1024
OUT_SUB_TILE = 512
STAT_WIDTH = V7X_LANES


def _alibi_slopes():
    n = N_DIL_GROUPS * N_HEADS
    s = 2.0 ** (-8.0 * np.arange(1, n + 1) / n)
    return s.reshape(N_DIL_GROUPS, N_HEADS)


def _layer_norm(y, gain, bias):
    mu = jnp.mean(y, axis=-1, keepdims=True)
    yc = y - mu
    var = jnp.mean(yc * yc, axis=-1, keepdims=True)
    return yc * lax.rsqrt(var + LN_EPS) * gain + bias


def _dot(a, b):
    return jnp.dot(a, b, preferred_element_type=jnp.float32)


def _const_spec(shape):
    return pl.BlockSpec(shape, lambda *_: (0,) * len(shape), pipeline_mode=pl.Buffered(1))


def _params(n_grid_dims):
    return pltpu.CompilerParams(
        dimension_semantics=("arbitrary",) * n_grid_dims,
        vmem_limit_bytes=V7X_VMEM_LIMIT_BYTES)


def _load_weight_bf16(src_hbm, dst_ref, stage_ref, sem):
    n_rows = stage_ref.shape[1]
    n_chunks = src_hbm.shape[0] // n_rows
    assert n_chunks * n_rows == src_hbm.shape[0]

    def chunk_copy(c):
        return pltpu.make_async_copy(src_hbm.at[pl.ds(c * n_rows, n_rows), :],
                                     stage_ref.at[c % 2], sem.at[c % 2])

    chunk_copy(0).start()
    for c in range(n_chunks):
        if c + 1 < n_chunks:
            chunk_copy(c + 1).start()
        chunk_copy(c).wait()
        dst_ref[c * n_rows:(c + 1) * n_rows, :] = stage_ref[c % 2].astype(jnp.bfloat16)


def _convert_specs(convert, n_steps, step_index):
    in_specs, out_specs, out_shapes = [], [], []
    for stack, layer in convert:
        _, rows, cols = stack.shape
        blk = rows // n_steps
        assert blk * n_steps == rows and blk % V7X_BF16_SUBLANES == 0
        in_specs.append(pl.BlockSpec(
            (None, blk, cols), lambda *ids, layer=layer: (layer, step_index(*ids), 0)))
        out_specs.append(pl.BlockSpec((blk, cols), lambda *ids: (step_index(*ids), 0)))
        out_shapes.append(jax.ShapeDtypeStruct((rows, cols), jnp.bfloat16))
    return in_specs, out_specs, out_shapes


def _ffn_ln_body(layer, n_conv, *refs):
    x_ref, wg_in, wu_in, wd_in, gain_ref, bias_ref = refs[:6]
    conv_in = refs[6:6 + n_conv]
    o_ref = refs[6 + n_conv]
    conv_out = refs[7 + n_conv:7 + 2 * n_conv]
    scratch = refs[7 + 2 * n_conv:]
    if layer is None:
        wg_ref, wu_ref, wd_ref = wg_in, wu_in, wd_in
    else:
        wg_ref, wu_ref, wd_ref, stage_in_ref, stage_down_ref, sem = scratch

        @pl.when(pl.program_id(0) == 0)
        def _():
            _load_weight_bf16(wg_in.at[layer], wg_ref, stage_in_ref, sem)
            _load_weight_bf16(wu_in.at[layer], wu_ref, stage_in_ref, sem)
            _load_weight_bf16(wd_in.at[layer], wd_ref, stage_down_ref, sem)

    for src, dst in zip(conv_in, conv_out):
        dst[...] = src[...].astype(jnp.bfloat16)

    for r0 in range(0, x_ref.shape[0], FFN_SUB_TILE):
        rows = slice(r0, r0 + FFN_SUB_TILE)
        x = x_ref[rows, :]
        xb = x.astype(jnp.bfloat16)
        h = None
        for lo, hi in FFN_CHUNKS:
            g = _dot(xb, wg_ref[:, lo:hi])
            u = _dot(xb, wu_ref[:, lo:hi])
            hg = 0.5 * g
            a = ((hg + hg * jnp.tanh(hg)) * u).astype(jnp.bfloat16)
            part = _dot(a, wd_ref[lo:hi, :])
            h = part if h is None else h + part
        y = ALPHA * x + MACARON_WEIGHT * h
        o_ref[rows, :] = _layer_norm(y, gain_ref[...], bias_ref[...])


def _ffn_ln(x, w_gate, w_up, w_down, gain, bias, layer=None, convert=()):
    m = x.shape[0]
    tm = FFN_ROW_TILE
    n_steps = m // tm
    row_spec = pl.BlockSpec((tm, D_MODEL), lambda i: (i, 0))
    if layer is None:
        weight_specs = [_const_spec((D_MODEL, D_FF)), _const_spec((D_MODEL, D_FF)),
                        _const_spec((D_FF, D_MODEL))]
        scratch = []
    else:
        weight_specs = [pl.BlockSpec(memory_space=pl.ANY)] * 3
        scratch = [pltpu.VMEM((D_MODEL, D_FF), jnp.bfloat16),
                   pltpu.VMEM((D_MODEL, D_FF), jnp.bfloat16),
                   pltpu.VMEM((D_FF, D_MODEL), jnp.bfloat16),
                   pltpu.VMEM((2, D_MODEL // WEIGHT_LOAD_CHUNKS, D_FF), jnp.float32),
                   pltpu.VMEM((2, D_FF // WEIGHT_LOAD_CHUNKS, D_MODEL), jnp.float32),
                   pltpu.SemaphoreType.DMA((2,))]
    conv_in, conv_out, conv_shapes = _convert_specs(convert, n_steps, lambda i: i)
    outs = pl.pallas_call(
        functools.partial(_ffn_ln_body, layer, len(convert)),
        grid=(n_steps,),
        in_specs=[row_spec] + weight_specs
                 + [_const_spec((1, D_MODEL)), _const_spec((1, D_MODEL))] + conv_in,
        out_specs=[row_spec] + conv_out,
        out_shape=[jax.ShapeDtypeStruct((m, D_MODEL), jnp.float32)] + conv_shapes,
        scratch_shapes=scratch,
        compiler_params=_params(1),
        name="ffn_ln",
    )(x, w_gate, w_up, w_down, gain, bias, *[stack for stack, _ in convert])
    return outs[0], outs[1:]


def _pool_ln_body(seq_len, xm_ref, xp_ref, xn_ref, win_ref, wgrp_ref, scale_ref, wout_ref,
                  gain_ref, bias_ref, conv_in_ref, o_ref, conv_out_ref, xe_ref, u_ref, lvl_ref):
    conv_out_ref[...] = conv_in_ref[...].astype(jnp.bfloat16)
    ts = xm_ref.shape[0]
    halo = POOL_HALO
    s = pl.program_id(1)
    n_s = pl.num_programs(1)
    xe_ref[0:halo, :] = xp_ref[...]
    xe_ref[halo:halo + ts, :] = xm_ref[...]
    xe_ref[halo + ts:, :] = xn_ref[...]
    tb = POOL_SUB_TILE
    n_ext = tb + 2 * halo
    n_sub = ts // tb
    for k in range(n_sub):
        r0 = k * tb
        u_ext = _dot(xe_ref[r0:r0 + n_ext, :].astype(jnp.bfloat16), win_ref[...])
        row = lax.broadcasted_iota(jnp.int32, (n_ext, 1), 0)
        if k == 0:
            u_ext = jnp.where((row < halo) & (s == 0), 0.0, u_ext)
        if k == n_sub - 1:
            u_ext = jnp.where((row >= halo + tb) & (s == n_s - 1), 0.0, u_ext)
        u_ref[k, 0:n_ext, :] = u_ext
        u_ref[k, n_ext:, :] = jnp.zeros((V7X_SUBLANES, D_MODEL), jnp.float32)

        t_abs = s * ts + r0 + lax.broadcasted_iota(jnp.int32, (tb, 1), 0)
        ys = []
        lvl = 0
        for g, w in enumerate(POOL_WINDOWS):
            half = w // 2
            cols = slice(g * POOL_GROUP_DIM, (g + 1) * POOL_GROUP_DIM)
            src = functools.partial(lambda k_, c_, lo, n: u_ref[k_, lo:lo + n, c_], k, cols)
            width = 1
            while 2 * width < w:
                lvl_ref[k, lvl, 0:n_ext, :] = src(0, n_ext) + src(width, n_ext)
                lvl_ref[k, lvl, n_ext:, :] = jnp.zeros((V7X_SUBLANES, POOL_GROUP_DIM), jnp.float32)
                src = functools.partial(lambda k_, l_, lo, n: lvl_ref[k_, l_, lo:lo + n, :], k, lvl)
                width, lvl = 2 * width, lvl + 1
            start = halo - half
            win_sum = src(start, tb) + src(start + width, tb)
            count = jnp.minimum(t_abs + half, seq_len) - jnp.maximum(t_abs - half, 0)
            mean = win_sum / count.astype(jnp.float32)
            mixed = (mean - u_ref[k, halo:halo + tb, cols]).astype(jnp.bfloat16)
            ys.append(_dot(mixed, wgrp_ref[g]))
        y = (jnp.concatenate(ys, axis=1) * scale_ref[...]).astype(jnp.bfloat16)
        m = _dot(y, wout_ref[...])
        o_ref[r0:r0 + tb, :] = _layer_norm(ALPHA * xm_ref[r0:r0 + tb, :] + m,
                                           gain_ref[...], bias_ref[...])


def _pool_ln(x, w_in, w_group, scale, w_out, gain, bias, convert):
    b, s_len, _ = x.shape
    ts = POOL_ROW_TILE
    halo = POOL_HALO
    blocks_per_tile = ts // halo
    n_halo_blocks = s_len // halo
    n_levels = sum(int(np.log2(w)) - 1 for w in POOL_WINDOWS)
    tb = POOL_SUB_TILE
    buf_rows = tb + 2 * halo + V7X_SUBLANES
    main_spec = pl.BlockSpec((None, ts, D_MODEL), lambda bi, si: (bi, si, 0))
    prev_spec = pl.BlockSpec(
        (None, halo, D_MODEL),
        lambda bi, si: (bi, jnp.maximum(si * blocks_per_tile - 1, 0), 0))
    next_spec = pl.BlockSpec(
        (None, halo, D_MODEL),
        lambda bi, si: (bi, jnp.minimum((si + 1) * blocks_per_tile, n_halo_blocks - 1), 0))
    n_s = s_len // ts
    conv_in, conv_out, conv_shapes = _convert_specs([convert], b * n_s, lambda bi, si: bi * n_s + si)
    outs = pl.pallas_call(
        functools.partial(_pool_ln_body, s_len),
        grid=(b, n_s),
        in_specs=[main_spec, prev_spec, next_spec,
                  _const_spec((D_MODEL, D_MODEL)),
                  _const_spec((N_POOL_GROUPS, POOL_GROUP_DIM, POOL_GROUP_DIM)),
                  _const_spec((1, D_MODEL)),
                  _const_spec((D_MODEL, D_MODEL)),
                  _const_spec((1, D_MODEL)), _const_spec((1, D_MODEL))] + conv_in,
        out_specs=[main_spec] + conv_out,
        out_shape=[jax.ShapeDtypeStruct(x.shape, jnp.float32)] + conv_shapes,
        scratch_shapes=[pltpu.VMEM((ts + 2 * halo, D_MODEL), jnp.float32),
                        pltpu.VMEM((ts // tb, buf_rows, D_MODEL), jnp.float32),
                        pltpu.VMEM((ts // tb, n_levels, buf_rows, POOL_GROUP_DIM), jnp.float32)],
        compiler_params=_params(2),
        name="pool_ln",
    )(x, x, x, w_in, w_group, scale, w_out, gain, bias, convert[0])
    return outs[0], outs[1:]


def _qkv_body(x_ref, w_ref, o0_ref, o1_ref, o2_ref, x_slabs):
    tm = x_ref.shape[0]
    n_slabs = D_MODEL // V7X_LANES
    for c in range(n_slabs):
        x_slabs[c] = x_ref[:, c * V7X_LANES:(c + 1) * V7X_LANES]
    ts = QKV_SUB_TILE
    for r0 in range(0, tm, ts):
        for g, ((_, d), o_ref) in enumerate(zip(DIL_CONFIGS, (o0_ref, o1_ref, o2_ref))):
            n = ts // d
            if d == 1:
                xp = x_ref[r0:r0 + ts, :]
            else:
                xp = jnp.concatenate(
                    [jnp.concatenate([x_slabs[c, pl.ds(r0 + r, n, stride=d), :] for r in range(d)],
                                     axis=0)
                     for c in range(n_slabs)], axis=1)
            xp = xp.astype(jnp.bfloat16)
            for j in range(3):
                lo = (g * 3 + j) * ATTN_WIDTH
                res = _dot(xp, w_ref[:, lo:lo + ATTN_WIDTH])
                if j == 0:
                    res = res * (HEAD_DIM ** -0.5 * LOG2_E)
                res = res.astype(jnp.bfloat16)
                for r in range(d):
                    o_ref[j, r, r0 // d:r0 // d + n, :] = res[r * n:(r + 1) * n]


def _qkv_proj(x, w_qkv, batch, seq_len):
    tm = QKV_ROW_TILE
    tiles_per_seq = seq_len // tm
    return pl.pallas_call(
        _qkv_body,
        grid=(batch * tiles_per_seq,),
        in_specs=[pl.BlockSpec((tm, D_MODEL), lambda i: (i, 0)),
                  _const_spec((D_MODEL, QKV_WIDTH))],
        out_specs=[pl.BlockSpec((3, None, d, tm // d, ATTN_WIDTH),
                                lambda i: (0, i // tiles_per_seq, 0, i % tiles_per_seq, 0))
                   for _, d in DIL_CONFIGS],
        out_shape=[jax.ShapeDtypeStruct((3, batch, d, seq_len // d, ATTN_WIDTH), jnp.bfloat16)
                   for _, d in DIL_CONFIGS],
        scratch_shapes=[pltpu.VMEM((D_MODEL // V7X_LANES, tm, V7X_LANES), jnp.float32)],
        compiler_params=_params(1),
        name="qkv_proj",
    )(x, w_qkv)


def _attn_bias(group):
    window, d = DIL_CONFIGS[group]
    qb, kb = ATTN_Q_BLOCK, ATTN_K_BLOCK
    r = (kb - qb) // 2
    a = np.arange(qb)[:, None]
    c = np.arange(kb)[None, :]
    rel = c - r - a
    band = np.abs(rel) <= r
    valid = np.stack([band, band & (c >= r), band & (c < qb + r)])
    neg_dist = np.where(valid, -(d * np.abs(rel)).astype(np.float64), MASK_VALUE).astype(np.float32)
    slopes = (_alibi_slopes()[group] * LOG2_E).astype(np.float32)
    return jnp.asarray(slopes[None, :, None, None] * neg_dist[:, None])


def _attn_body(n_blocks_total, q_ref, k_ref, kp_ref, kn_ref, v_ref, vp_ref, vn_ref, bias_ref,
               o_ref, stat_ref, ke_ref, ve_ref, s_ref, p_ref):
    n_res, tq, _ = q_ref.shape
    qb, kb = ATTN_Q_BLOCK, ATTN_K_BLOCK
    r = (kb - qb) // 2
    hq = qb // 2
    n_blk = tq // qb
    ti = pl.program_id(2)
    ke_ref[:, 0:r, :] = kp_ref[...]
    ke_ref[:, r:r + tq, :] = k_ref[...]
    ke_ref[:, r + tq:, :] = kn_ref[...]
    for pair in range(N_HEADS // 2):
        src = slice(pair * V7X_LANES, (pair + 1) * V7X_LANES)
        dst = slice(2 * pair * V7X_LANES, (2 * pair + 1) * V7X_LANES)
        one = slice((2 * pair + 1) * V7X_LANES, (2 * pair + 2) * V7X_LANES)
        ve_ref[:, 0:r, dst] = vp_ref[:, :, src]
        ve_ref[:, r:r + tq, dst] = v_ref[:, :, src]
        ve_ref[:, r + tq:, dst] = vn_ref[:, :, src]
        ve_ref[:, :, one] = jnp.ones((n_res, tq + 2 * r, V7X_LANES), jnp.bfloat16)

    low_half = lax.broadcasted_iota(jnp.int32, (qb, V7X_LANES), 1) < HEAD_DIM

    for res, blk in [(a, b) for a in range(n_res) for b in range(n_blk)]:
        row0 = blk * qb
        g_blk = ti * n_blk + blk
        variant = jnp.where(g_blk == 0, 1, 0) + jnp.where(g_blk == n_blocks_total - 1, 2, 0)
        stat_ref[res, pl.ds(row0, qb), :] = jnp.zeros((qb, STAT_WIDTH), jnp.float32)
        for pair in range(N_HEADS // 2):
            cols = slice(pair * V7X_LANES, (pair + 1) * V7X_LANES)
            buf = pair % ATTN_SCORE_BUFFERS
            q2 = q_ref[res, pl.ds(row0, qb), cols]
            zero = jnp.zeros_like(q2)
            q_stack = jnp.concatenate(
                [jnp.where(low_half, q2, zero), jnp.where(low_half, zero, q2)], axis=0)
            s_ref[buf] = lax.dot_general(q_stack, ke_ref[res, pl.ds(row0, kb), cols],
                                         (((1,), (1,)), ((), ())),
                                         preferred_element_type=jnp.float32)
            for c in range(4):
                head, half = 2 * pair + c // 2, c % 2
                rows = slice(c * hq, (c + 1) * hq)
                sc = s_ref[buf, rows, :] + bias_ref[variant, head, half * hq:(half + 1) * hq, :]
                m = jnp.max(sc, axis=1, keepdims=True)
                p_ref[buf, rows, :] = jnp.exp2(sc - m).astype(jnp.bfloat16)
                stat_ref[res, pl.ds(row0 + half * hq, hq), head:head + 1] = m
            v_cols = slice(2 * pair * V7X_LANES, (2 * pair + 2) * V7X_LANES)
            o2 = _dot(p_ref[buf], ve_ref[res, pl.ds(row0, kb), v_cols])
            o_ref[res, pl.ds(row0, qb), cols] = jnp.where(
                low_half, o2[:qb, :V7X_LANES], o2[qb:, :V7X_LANES]).astype(o_ref.dtype)
            for hh in range(2):
                lane_l = N_HEADS + 2 * pair + hh
                stat_ref[res, pl.ds(row0, qb), lane_l:lane_l + 1] = (
                    o2[hh * qb:(hh + 1) * qb, V7X_LANES + lane_l:V7X_LANES + lane_l + 1])


def _attn_group(qkv_g, group):
    window, d = DIL_CONFIGS[group]
    _, batch, _, sub_len, _ = qkv_g.shape
    qb, kb = ATTN_Q_BLOCK, ATTN_K_BLOCK
    r = window // (2 * d)
    assert r == (kb - qb) // 2 and sub_len >= 2 * qb
    tq = min(ATTN_Q_TILE, sub_len)
    n_tiles = sub_len // tq
    n_res = min(d, ATTN_Q_TILE // tq)
    halo_per_tile = tq // r
    n_halo_blocks = sub_len // r

    def main_spec(j):
        return pl.BlockSpec((None, None, n_res, tq, ATTN_WIDTH),
                            lambda bi, ri, ti: (j, bi, ri, ti, 0))

    def prev_spec(j):
        return pl.BlockSpec(
            (None, None, n_res, r, ATTN_WIDTH),
            lambda bi, ri, ti: (j, bi, ri, jnp.maximum(ti * halo_per_tile - 1, 0), 0))

    def next_spec(j):
        return pl.BlockSpec(
            (None, None, n_res, r, ATTN_WIDTH),
            lambda bi, ri, ti: (j, bi, ri, jnp.minimum((ti + 1) * halo_per_tile, n_halo_blocks - 1),
                                0))

    def out_spec(width):
        return pl.BlockSpec((None, n_res, tq, width), lambda bi, ri, ti: (bi, ri, ti, 0))

    return pl.pallas_call(
        functools.partial(_attn_body, sub_len // qb),
        grid=(batch, d // n_res, n_tiles),
        in_specs=[main_spec(0),
                  main_spec(1), prev_spec(1), next_spec(1),
                  main_spec(2), prev_spec(2), next_spec(2),
                  _const_spec((3, N_HEADS, qb, kb))],
        out_specs=[out_spec(ATTN_WIDTH), out_spec(STAT_WIDTH)],
        out_shape=[jax.ShapeDtypeStruct((batch, d, sub_len, ATTN_WIDTH), jnp.bfloat16),
                   jax.ShapeDtypeStruct((batch, d, sub_len, STAT_WIDTH), jnp.float32)],
        scratch_shapes=[pltpu.VMEM((n_res, tq + 2 * r, ATTN_WIDTH), jnp.bfloat16),
                        pltpu.VMEM((n_res, tq + 2 * r, 2 * ATTN_WIDTH), jnp.bfloat16),
                        pltpu.VMEM((ATTN_SCORE_BUFFERS, 2 * qb, kb), jnp.float32),
                        pltpu.VMEM((ATTN_SCORE_BUFFERS, 2 * qb, kb), jnp.bfloat16)],
        compiler_params=_params(3),
        name=f"attn_group{group}",
    )(*([qkv_g] * 7), _attn_bias(group))


def _head_expand_matrix():
    k = np.arange(2 * STAT_WIDTH)[:, None] % STAT_WIDTH
    c = np.arange(ATTN_WIDTH)[None, :] // HEAD_DIM
    return jnp.asarray(k == c, dtype=jnp.bfloat16)


def _attn_out_ln_body(x_ref, o0_ref, o1_ref, o2_ref, l0_ref, l1_ref, l2_ref, e_ref, w_ref,
                      gain_ref, bias_ref, out_ref, o_scr, l_scr, pad_scr):
    o_refs = (o0_ref, o1_ref, o2_ref)
    l_refs = (l0_ref, l1_ref, l2_ref)

    def token_order(ref, scr, g, r0, ts):
        assert DIL_CONFIGS[0][1] == 1 and all(dil > 1 for _, dil in DIL_CONFIGS[1:])
        d, _, width = ref.shape
        if d == 1:
            return ref[0, r0:r0 + ts, :].astype(jnp.float32)
        n_slabs = width // V7X_LANES
        n = ts // d
        if d % V7X_SUBLANES == 0 and scr is o_scr:
            pitch = n + V7X_SUBLANES
            base = (r0 // ts) * d * pitch
            for r in range(d):
                piece = ref[r, r0 // d:(r0 + ts) // d, :].astype(jnp.float32)
                for c in range(n_slabs):
                    pad_scr[c, base + r * pitch:base + r * pitch + n, :] = (
                        piece[:, c * V7X_LANES:(c + 1) * V7X_LANES])
            return jnp.concatenate(
                [jnp.concatenate([pad_scr[c, pl.ds(base + i, d, stride=pitch), :] for i in range(n)],
                                 axis=0)
                 for c in range(n_slabs)], axis=1)
        for r in range(d):
            piece = ref[r, r0 // d:(r0 + ts) // d, :].astype(jnp.float32)
            for c in range(n_slabs):
                scr[g - 1, c, pl.ds(r0 + r, n, stride=d), :] = (
                    piece[:, c * V7X_LANES:(c + 1) * V7X_LANES])
        return jnp.concatenate([scr[g - 1, c, r0:r0 + ts, :] for c in range(n_slabs)], axis=1)

    ts = OUT_SUB_TILE
    is_head = lax.broadcasted_iota(jnp.int32, (ts, STAT_WIDTH), 1) < N_HEADS
    for r0 in range(0, x_ref.shape[0], ts):
        ms = [token_order(l_refs[g], l_scr, g, r0, ts) for g in range(N_DIL_GROUPS)]
        ls = [pltpu.roll(m, STAT_WIDTH - N_HEADS, 1) for m in ms]
        top = jnp.maximum(jnp.maximum(ms[0], ms[1]), ms[2])
        es = [jnp.exp2(m - top) for m in ms]
        inv = 1.0 / (es[0] * ls[0] + es[1] * ls[1] + es[2] * ls[2])
        acc = None
        for g in range(N_DIL_GROUPS):
            wt = jnp.where(is_head, es[g] * inv, 0.0)
            hi = wt.astype(jnp.bfloat16)
            lo = (wt - hi.astype(jnp.float32)).astype(jnp.bfloat16)
            w_full = _dot(jnp.concatenate([hi, lo], axis=1), e_ref[...])
            term = w_full * token_order(o_refs[g], o_scr, g, r0, ts)
            acc = term if acc is None else acc + term
        m = _dot(acc.astype(jnp.bfloat16), w_ref[...])
        out_ref[r0:r0 + ts, :] = _layer_norm(ALPHA * x_ref[r0:r0 + ts, :] + m,
                                             gain_ref[...], bias_ref[...])


def _attn_out_ln(x, os_, stats, w_out, gain, bias, seq_len):
    m = x.shape[0]
    tm = OUT_ROW_TILE
    tiles_per_seq = seq_len // tm
    row_spec = pl.BlockSpec((tm, D_MODEL), lambda i: (i, 0))
    d_max = max(d for _, d in DIL_CONFIGS)
    pad_rows = tm + (tm // OUT_SUB_TILE) * d_max * V7X_SUBLANES

    def group_spec(d, width):
        return pl.BlockSpec((None, d, tm // d, width),
                            lambda i: (i // tiles_per_seq, 0, i % tiles_per_seq, 0))

    return pl.pallas_call(
        _attn_out_ln_body,
        grid=(m // tm,),
        in_specs=[row_spec]
                 + [group_spec(d, ATTN_WIDTH) for _, d in DIL_CONFIGS]
                 + [group_spec(d, STAT_WIDTH) for _, d in DIL_CONFIGS]
                 + [_const_spec((2 * STAT_WIDTH, ATTN_WIDTH)),
                    _const_spec((ATTN_WIDTH, D_MODEL)),
                    _const_spec((1, D_MODEL)), _const_spec((1, D_MODEL))],
        out_specs=row_spec,
        out_shape=jax.ShapeDtypeStruct((m, D_MODEL), jnp.float32),
        scratch_shapes=[
            pltpu.VMEM((N_DIL_GROUPS - 1, ATTN_WIDTH // V7X_LANES, tm, V7X_LANES), jnp.float32),
            pltpu.VMEM((N_DIL_GROUPS - 1, STAT_WIDTH // V7X_LANES, tm, V7X_LANES), jnp.float32),
            pltpu.VMEM((ATTN_WIDTH // V7X_LANES, pad_rows, V7X_LANES), jnp.float32)],
        compiler_params=_params(1),
        name="attn_out_ln",
    )(x, *os_, *stats, _head_expand_matrix(), w_out, gain, bias)


def kernel(x, ffn1_w_gate, ffn1_w_up, ffn1_w_down, ffn2_w_gate, ffn2_w_up, ffn2_w_down,
           ln_gain, ln_bias, pool_w_in, pool_w_group, pool_scale, pool_w_out,
           attn_w_qkv, attn_w_out):
    batch, seq_len, _ = x.shape
    n_mixers = 2
    h = x.reshape(batch * seq_len, D_MODEL)

    def ln_params(i, j):
        return ln_gain[i, j].reshape(1, D_MODEL), ln_bias[i, j].reshape(1, D_MODEL)

    ffn_stacks = [(ffn1_w_gate, ffn1_w_up, ffn1_w_down), (ffn2_w_gate, ffn2_w_up, ffn2_w_down)]
    ffn_order = [(which, i) for i in range(DEPTH) for which in range(2)]
    ffn_ready = [None]

    def ffn(k, h, gain, bias, extra=()):
        which, layer = ffn_order[k]
        convert = []
        if k + 1 < len(ffn_order):
            nxt_which, nxt_layer = ffn_order[k + 1]
            convert = [(w, nxt_layer) for w in ffn_stacks[nxt_which]]
        n_next = len(convert)
        convert = convert + list(extra)
        if ffn_ready[0] is None:
            h, conv = _ffn_ln(h, *ffn_stacks[which], gain, bias, layer=layer, convert=convert)
        else:
            h, conv = _ffn_ln(h, *ffn_ready[0], gain, bias, convert=convert)
        ffn_ready[0] = conv[:n_next]
        return h, conv[n_next:]

    w_qkv_ready = {}
    for i in range(DEPTH):
        if i % n_mixers == 0:
            p = i // n_mixers
            group_stack = pool_w_group.reshape(-1, D_MODEL, POOL_GROUP_DIM)
            h, (w_in, w_group, w_out) = ffn(2 * i, h, *ln_params(i, 0),
                                            [(pool_w_in, p), (group_stack, p), (pool_w_out, p)])
            assert i + 1 < DEPTH
            h, (w_qkv_ready[(i + 1) // n_mixers],) = _pool_ln(
                h.reshape(batch, seq_len, D_MODEL), w_in,
                w_group.reshape(N_POOL_GROUPS, POOL_GROUP_DIM, POOL_GROUP_DIM),
                pool_scale[p].reshape(1, D_MODEL), w_out, *ln_params(i, 1),
                (attn_w_qkv, (i + 1) // n_mixers))
            h = h.reshape(batch * seq_len, D_MODEL)
        else:
            a = i // n_mixers
            h, (w_out,) = ffn(2 * i, h, *ln_params(i, 0), [(attn_w_out, a)])
            qkv = _qkv_proj(h, w_qkv_ready[a], batch, seq_len)
            outs = [_attn_group(qkv[g], g) for g in range(N_DIL_GROUPS)]
            h = _attn_out_ln(h, [o for o, _ in outs], [l for _, l in outs],
                             w_out, *ln_params(i, 1), seq_len)
        h, _ = ffn(2 * i + 1, h, *ln_params(i, 2))
    return h.reshape(batch, seq_len, D_MODEL)
```

```python
import functools

import numpy as np
import jax
import jax.numpy as jnp
from jax import lax
from jax.experimental import pallas as pl
from jax.experimental.pallas import tpu as pltpu

D_MODEL = 1024
DEPTH = 2
POOL_WINDOWS = (2, 4, 8, 16)
N_POOL_GROUPS = len(POOL_WINDOWS)
POOL_GROUP_DIM = D_MODEL // N_POOL_GROUPS
POOL_HALO = max(POOL_WINDOWS) // 2
HEAD_DIM = 64
N_HEADS = D_MODEL // HEAD_DIM
DIL_CONFIGS = ((128, 1), (512, 4), (2048, 16))
N_DIL_GROUPS = len(DIL_CONFIGS)
ATTN_WIDTH = N_HEADS * HEAD_DIM
QKV_WIDTH = N_DIL_GROUPS * 3 * ATTN_WIDTH
D_FF = 2816
MACARON_WEIGHT = 0.5
ALPHA = (2.0 * DEPTH) ** 0.25
LN_EPS = 1e-5
MASK_VALUE = -1e30
LOG2_E = float(np.log2(np.e))

V7X_LANES = 128
V7X_MXU_DIM = 256
V7X_SUBLANES = 8
V7X_BF16_SUBLANES = 16
V7X_VMEM_LIMIT_BYTES = 56 * 1024 * 1024

FFN_ROW_TILE = 1024
FFN_SUB_TILE = 256
FFN_CHUNKS = ((0, 6 * V7X_MXU_DIM), (6 * V7X_MXU_DIM, D_FF))
WEIGHT_LOAD_CHUNKS = 8
POOL_ROW_TILE = 1024
POOL_SUB_TILE = 256
QKV_ROW_TILE = 512
QKV_SUB_TILE = 512
ATTN_Q_TILE = 1024
ATTN_Q_BLOCK = 128
ATTN_K_BLOCK = 256
ATTN_SCORE_BUFFERS = 2
OUT_ROW_TILE = 1024
OUT_SUB_TILE = 512
STAT_WIDTH = V7X_LANES


def _alibi_slopes():
    n = N_DIL_GROUPS * N_HEADS
    s = 2.0 ** (-8.0 * np.arange(1, n + 1) / n)
    return s.reshape(N_DIL_GROUPS, N_HEADS)


def _layer_norm(y, gain, bias):
    mu = jnp.mean(y, axis=-1, keepdims=True)
    yc = y - mu
    var = jnp.mean(yc * yc, axis=-1, keepdims=True)
    return yc * lax.rsqrt(var + LN_EPS) * gain + bias


def _dot(a, b):
    return jnp.dot(a, b, preferred_element_type=jnp.float32)


def _const_spec(shape):
    return pl.BlockSpec(shape, lambda *_: (0,) * len(shape), pipeline_mode=pl.Buffered(1))


def _params(n_grid_dims):
    return pltpu.CompilerParams(
        dimension_semantics=("arbitrary",) * n_grid_dims,
        vmem_limit_bytes=V7X_VMEM_LIMIT_BYTES)


def _load_weight_bf16(src_hbm, dst_ref, stage_ref, sem):
    n_rows = stage_ref.shape[1]
    n_chunks = src_hbm.shape[0] // n_rows
    assert n_chunks * n_rows == src_hbm.shape[0]

    def chunk_copy(c):
        return pltpu.make_async_copy(src_hbm.at[pl.ds(c * n_rows, n_rows), :],
                                     stage_ref.at[c % 2], sem.at[c % 2])

    chunk_copy(0).start()
    for c in range(n_chunks):
        if c + 1 < n_chunks:
            chunk_copy(c + 1).start()
        chunk_copy(c).wait()
        dst_ref[c * n_rows:(c + 1) * n_rows, :] = stage_ref[c % 2].astype(jnp.bfloat16)


def _convert_specs(convert, n_steps, step_index):
    in_specs, out_specs, out_shapes = [], [], []
    for stack, layer in convert:
        _, rows, cols = stack.shape
        blk = rows // n_steps
        assert blk * n_steps == rows and blk % V7X_BF16_SUBLANES == 0
        in_specs.append(pl.BlockSpec(
            (None, blk, cols), lambda *ids, layer=layer: (layer, step_index(*ids), 0)))
        out_specs.append(pl.BlockSpec((blk, cols), lambda *ids: (step_index(*ids), 0)))
        out_shapes.append(jax.ShapeDtypeStruct((rows, cols), jnp.bfloat16))
    return in_specs, out_specs, out_shapes


def _ffn_ln_body(layer, n_conv, *refs):
    x_ref, wg_in, wu_in, wd_in, gain_ref, bias_ref = refs[:6]
    conv_in = refs[6:6 + n_conv]
    o_ref = refs[6 + n_conv]
    conv_out = refs[7 + n_conv:7 + 2 * n_conv]
    scratch = refs[7 + 2 * n_conv:]
    if layer is None:
        wg_ref, wu_ref, wd_ref = wg_in, wu_in, wd_in
    else:
        wg_ref, wu_ref, wd_ref, stage_in_ref, stage_down_ref, sem = scratch

        @pl.when(pl.program_id(0) == 0)
        def _():
            _load_weight_bf16(wg_in.at[layer], wg_ref, stage_in_ref, sem)
            _load_weight_bf16(wu_in.at[layer], wu_ref, stage_in_ref, sem)
            _load_weight_bf16(wd_in.at[layer], wd_ref, stage_down_ref, sem)

    for src, dst in zip(conv_in, conv_out):
        dst[...] = src[...].astype(jnp.bfloat16)

    for r0 in range(0, x_ref.shape[0], FFN_SUB_TILE):
        rows = slice(r0, r0 + FFN_SUB_TILE)
        x = x_ref[rows, :]
        xb = x.astype(jnp.bfloat16)
        h = None
        for lo, hi in FFN_CHUNKS:
            g = _dot(xb, wg_ref[:, lo:hi])
            u = _dot(xb, wu_ref[:, lo:hi])
            hg = 0.5 * g
            a = ((hg + hg * jnp.tanh(hg)) * u).astype(jnp.bfloat16)
            part = _dot(a, wd_ref[lo:hi, :])
            h = part if h is None else h + part
        y = ALPHA * x + MACARON_WEIGHT * h
        o_ref[rows, :] = _layer_norm(y, gain_ref[...], bias_ref[...])


def _ffn_ln(x, w_gate, w_up, w_down, gain, bias, layer=None, convert=()):
    m = x.shape[0]
    tm = FFN_ROW_TILE
    n_steps = m // tm
    row_spec = pl.BlockSpec((tm, D_MODEL), lambda i: (i, 0))
    if layer is None:
        weight_specs = [_const_spec((D_MODEL, D_FF)), _const_spec((D_MODEL, D_FF)),
                        _const_spec((D_FF, D_MODEL))]
        scratch = []
    else:
        weight_specs = [pl.BlockSpec(memory_space=pl.ANY)] * 3
        scratch = [pltpu.VMEM((D_MODEL, D_FF), jnp.bfloat16),
                   pltpu.VMEM((D_MODEL, D_FF), jnp.bfloat16),
                   pltpu.VMEM((D_FF, D_MODEL), jnp.bfloat16),
                   pltpu.VMEM((2, D_MODEL // WEIGHT_LOAD_CHUNKS, D_FF), jnp.float32),
                   pltpu.VMEM((2, D_FF // WEIGHT_LOAD_CHUNKS, D_MODEL), jnp.float32),
                   pltpu.SemaphoreType.DMA((2,))]
    conv_in, conv_out, conv_shapes = _convert_specs(convert, n_steps, lambda i: i)
    outs = pl.pallas_call(
        functools.partial(_ffn_ln_body, layer, len(convert)),
        grid=(n_steps,),
        in_specs=[row_spec] + weight_specs
                 + [_const_spec((1, D_MODEL)), _const_spec((1, D_MODEL))] + conv_in,
        out_specs=[row_spec] + conv_out,
        out_shape=[jax.ShapeDtypeStruct((m, D_MODEL), jnp.float32)] + conv_shapes,
        scratch_shapes=scratch,
        compiler_params=_params(1),
        name="ffn_ln",
    )(x, w_gate, w_up, w_down, gain, bias, *[stack for stack, _ in convert])
    return outs[0], outs[1:]


def _pool_ln_body(seq_len, xm_ref, xp_ref, xn_ref, win_ref, wgrp_ref, scale_ref, wout_ref,
                  gain_ref, bias_ref, conv_in_ref, o_ref, conv_out_ref, xe_ref, u_ref, lvl_ref):
    conv_out_ref[...] = conv_in_ref[...].astype(jnp.bfloat16)
    ts = xm_ref.shape[0]
    halo = POOL_HALO
    s = pl.program_id(1)
    n_s = pl.num_programs(1)
    xe_ref[0:halo, :] = xp_ref[...]
    xe_ref[halo:halo + ts, :] = xm_ref[...]
    xe_ref[halo + ts:, :] = xn_ref[...]
    tb = POOL_SUB_TILE
    n_ext = tb + 2 * halo
    n_sub = ts // tb
    for k in range(n_sub):
        r0 = k * tb
        u_ext = _dot(xe_ref[r0:r0 + n_ext, :].astype(jnp.bfloat16), win_ref[...])
        row = lax.broadcasted_iota(jnp.int32, (n_ext, 1), 0)
        if k == 0:
            u_ext = jnp.where((row < halo) & (s == 0), 0.0, u_ext)
        if k == n_sub - 1:
            u_ext = jnp.where((row >= halo + tb) & (s == n_s - 1), 0.0, u_ext)
        u_ref[k, 0:n_ext, :] = u_ext
        u_ref[k, n_ext:, :] = jnp.zeros((V7X_SUBLANES, D_MODEL), jnp.float32)

        t_abs = s * ts + r0 + lax.broadcasted_iota(jnp.int32, (tb, 1), 0)
        ys = []
        lvl = 0
        for g, w in enumerate(POOL_WINDOWS):
            half = w // 2
            cols = slice(g * POOL_GROUP_DIM, (g + 1) * POOL_GROUP_DIM)
            src = functools.partial(lambda k_, c_, lo, n: u_ref[k_, lo:lo + n, c_], k, cols)
            width = 1
            while 2 * width < w:
                lvl_ref[k, lvl, 0:n_ext, :] = src(0, n_ext) + src(width, n_ext)
                lvl_ref[k, lvl, n_ext:, :] = jnp.zeros((V7X_SUBLANES, POOL_GROUP_DIM), jnp.float32)
                src = functools.partial(lambda k_, l_, lo, n: lvl_ref[k_, l_, lo:lo + n, :], k, lvl)
                width, lvl = 2 * width, lvl + 1
            start = halo - half
            win_sum = src(start, tb) + src(start + width, tb)
            count = jnp.minimum(t_abs + half, seq_len) - jnp.maximum(t_abs - half, 0)
            mean = win_sum / count.astype(jnp.float32)
            mixed = (mean - u_ref[k, halo:halo + tb, cols]).astype(jnp.bfloat16)
            ys.append(_dot(mixed, wgrp_ref[g]))
        y = (jnp.concatenate(ys, axis=1) * scale_ref[...]).astype(jnp.bfloat16)
        m = _dot(y, wout_ref[...])
        o_ref[r0:r0 + tb, :] = _layer_norm(ALPHA * xm_ref[r0:r0 + tb, :] + m,
                                           gain_ref[...], bias_ref[...])


def _pool_ln(x, w_in, w_group, scale, w_out, gain, bias, convert):
    b, s_len, _ = x.shape
    ts = POOL_ROW_TILE
    halo = POOL_HALO
    blocks_per_tile = ts // halo
    n_halo_blocks = s_len // halo
    n_levels = sum(int(np.log2(w)) - 1 for w in POOL_WINDOWS)
    tb = POOL_SUB_TILE
    buf_rows = tb + 2 * halo + V7X_SUBLANES
    main_spec = pl.BlockSpec((None, ts, D_MODEL), lambda bi, si: (bi, si, 0))
    prev_spec = pl.BlockSpec(
        (None, halo, D_MODEL),
        lambda bi, si: (bi, jnp.maximum(si * blocks_per_tile - 1, 0), 0))
    next_spec = pl.BlockSpec(
        (None, halo, D_MODEL),
        lambda bi, si: (bi, jnp.minimum((si + 1) * blocks_per_tile, n_halo_blocks - 1), 0))
    n_s = s_len // ts
    conv_in, conv_out, conv_shapes = _convert_specs([convert], b * n_s, lambda bi, si: bi * n_s + si)
    outs = pl.pallas_call(
        functools.partial(_pool_ln_body, s_len),
        grid=(b, n_s),
        in_specs=[main_spec, prev_spec, next_spec,
                  _const_spec((D_MODEL, D_MODEL)),
                  _const_spec((N_POOL_GROUPS, POOL_GROUP_DIM, POOL_GROUP_DIM)),
                  _const_spec((1, D_MODEL)),
                  _const_spec((D_MODEL, D_MODEL)),
                  _const_spec((1, D_MODEL)), _const_spec((1, D_MODEL))] + conv_in,
        out_specs=[main_spec] + conv_out,
        out_shape=[jax.ShapeDtypeStruct(x.shape, jnp.float32)] + conv_shapes,
        scratch_shapes=[pltpu.VMEM((ts + 2 * halo, D_MODEL), jnp.float32),
                        pltpu.VMEM((ts // tb, buf_rows, D_MODEL), jnp.float32),
                        pltpu.VMEM((ts // tb, n_levels, buf_rows, POOL_GROUP_DIM), jnp.float32)],
        compiler_params=_params(2),
        name="pool_ln",
    )(x, x, x, w_in, w_group, scale, w_out, gain, bias, convert[0])
    return outs[0], outs[1:]


def _qkv_body(x_ref, w_ref, o0_ref, o1_ref, o2_ref, x_slabs):
    tm = x_ref.shape[0]
    n_slabs = D_MODEL // V7X_LANES
    for c in range(n_slabs):
        x_slabs[c] = x_ref[:, c * V7X_LANES:(c + 1) * V7X_LANES]
    ts = QKV_SUB_TILE
    for r0 in range(0, tm, ts):
        for g, ((_, d), o_ref) in enumerate(zip(DIL_CONFIGS, (o0_ref, o1_ref, o2_ref))):
            n = ts // d
            if d == 1:
                xp = x_ref[r0:r0 + ts, :]
            else:
                xp = jnp.concatenate(
                    [jnp.concatenate([x_slabs[c, pl.ds(r0 + r, n, stride=d), :] for r in range(d)],
                                     axis=0)
                     for c in range(n_slabs)], axis=1)
            xp = xp.astype(jnp.bfloat16)
            for j in range(3):
                lo = (g * 3 + j) * ATTN_WIDTH
                res = _dot(xp, w_ref[:, lo:lo + ATTN_WIDTH])
                if j == 0:
                    res = res * (HEAD_DIM ** -0.5 * LOG2_E)
                res = res.astype(jnp.bfloat16)
                for r in range(d):
                    o_ref[j, r, r0 // d:r0 // d + n, :] = res[r * n:(r + 1) * n]


def _qkv_proj(x, w_qkv, batch, seq_len):
    tm = QKV_ROW_TILE
    tiles_per_seq = seq_len // tm
    return pl.pallas_call(
        _qkv_body,
        grid=(batch * tiles_per_seq,),
        in_specs=[pl.BlockSpec((tm, D_MODEL), lambda i: (i, 0)),
                  _const_spec((D_MODEL, QKV_WIDTH))],
        out_specs=[pl.BlockSpec((3, None, d, tm // d, ATTN_WIDTH),
                                lambda i: (0, i // tiles_per_seq, 0, i % tiles_per_seq, 0))
                   for _, d in DIL_CONFIGS],
        out_shape=[jax.ShapeDtypeStruct((3, batch, d, seq_len // d, ATTN_WIDTH), jnp.bfloat16)
                   for _, d in DIL_CONFIGS],
        scratch_shapes=[pltpu.VMEM((D_MODEL // V7X_LANES, tm, V7X_LANES), jnp.float32)],
        compiler_params=_params(1),
        name="qkv_proj",
    )(x, w_qkv)


def _attn_bias(group):
    window, d = DIL_CONFIGS[group]
    qb, kb = ATTN_Q_BLOCK, ATTN_K_BLOCK
    r = (kb - qb) // 2
    a = np.arange(qb)[:, None]
    c = np.arange(kb)[None, :]
    rel = c - r - a
    band = np.abs(rel) <= r
    valid = np.stack([band, band & (c >= r), band & (c < qb + r)])
    neg_dist = np.where(valid, -(d * np.abs(rel)).astype(np.float64), MASK_VALUE).astype(np.float32)
    slopes = (_alibi_slopes()[group] * LOG2_E).astype(np.float32)
    return jnp.asarray(slopes[None, :, None, None] * neg_dist[:, None])


def _attn_body(n_blocks_total, has_halo, q_ref, k_ref, *refs):
    if has_halo:
        kp_ref, kn_ref, v_ref, vp_ref, vn_ref = refs[:5]
        refs = refs[5:]
    else:
        v_ref = refs[0]
        refs = refs[1:]
    bias_ref, o_ref, stat_ref, ke_ref, ve_ref, s_ref, p_ref = refs
    n_res, tq, _ = q_ref.shape
    qb, kb = ATTN_Q_BLOCK, ATTN_K_BLOCK
    r = (kb - qb) // 2
    hq = qb // 2
    n_blk = tq // qb
    ti = pl.program_id(2)
    halo_zeros = jnp.zeros((n_res, r, ATTN_WIDTH), jnp.bfloat16)
    ke_ref[:, 0:r, :] = kp_ref[...] if has_halo else halo_zeros
    ke_ref[:, r:r + tq, :] = k_ref[...]
    ke_ref[:, r + tq:, :] = kn_ref[...] if has_halo else halo_zeros
    for pair in range(N_HEADS // 2):
        src = slice(pair * V7X_LANES, (pair + 1) * V7X_LANES)
        dst = slice(2 * pair * V7X_LANES, (2 * pair + 1) * V7X_LANES)
        one = slice((2 * pair + 1) * V7X_LANES, (2 * pair + 2) * V7X_LANES)
        ve_ref[:, 0:r, dst] = vp_ref[:, :, src] if has_halo else halo_zeros[:, :, src]
        ve_ref[:, r:r + tq, dst] = v_ref[:, :, src]
        ve_ref[:, r + tq:, dst] = vn_ref[:, :, src] if has_halo else halo_zeros[:, :, src]
        ve_ref[:, :, one] = jnp.ones((n_res, tq + 2 * r, V7X_LANES), jnp.bfloat16)

    low_half = lax.broadcasted_iota(jnp.int32, (qb, V7X_LANES), 1) < HEAD_DIM

    for res, blk in [(a, b) for a in range(n_res) for b in range(n_blk)]:
        row0 = blk * qb
        g_blk = ti * n_blk + blk
        variant = jnp.where(g_blk == 0, 1, 0) + jnp.where(g_blk == n_blocks_total - 1, 2, 0)
        stat_ref[res, pl.ds(row0, qb), :] = jnp.zeros((qb, STAT_WIDTH), jnp.float32)
        for pair in range(N_HEADS // 2):
            cols = slice(pair * V7X_LANES, (pair + 1) * V7X_LANES)
            buf = pair % ATTN_SCORE_BUFFERS
            q2 = q_ref[res, pl.ds(row0, qb), cols]
            zero = jnp.zeros_like(q2)
            q_stack = jnp.concatenate(
                [jnp.where(low_half, q2, zero), jnp.where(low_half, zero, q2)], axis=0)
            s_ref[buf] = lax.dot_general(q_stack, ke_ref[res, pl.ds(row0, kb), cols],
                                         (((1,), (1,)), ((), ())),
                                         preferred_element_type=jnp.float32)
            for c in range(4):
                head, half = 2 * pair + c // 2, c % 2
                rows = slice(c * hq, (c + 1) * hq)
                sc = s_ref[buf, rows, :] + bias_ref[variant, head, half * hq:(half + 1) * hq, :]
                m = jnp.max(sc, axis=1, keepdims=True)
                p_ref[buf, rows, :] = jnp.exp2(sc - m).astype(jnp.bfloat16)
                stat_ref[res, pl.ds(row0 + half * hq, hq), head:head + 1] = m
            v_cols = slice(2 * pair * V7X_LANES, (2 * pair + 2) * V7X_LANES)
            o2 = _dot(p_ref[buf], ve_ref[res, pl.ds(row0, kb), v_cols])
            o_ref[res, pl.ds(row0, qb), cols] = jnp.where(
                low_half, o2[:qb, :V7X_LANES], o2[qb:, :V7X_LANES]).astype(o_ref.dtype)
            for hh in range(2):
                lane_l = N_HEADS + 2 * pair + hh
                stat_ref[res, pl.ds(row0, qb), lane_l:lane_l + 1] = (
                    o2[hh * qb:(hh + 1) * qb, V7X_LANES + lane_l:V7X_LANES + lane_l + 1])


def _attn_group(qkv_g, group):
    window, d = DIL_CONFIGS[group]
    _, batch, _, sub_len, _ = qkv_g.shape
    qb, kb = ATTN_Q_BLOCK, ATTN_K_BLOCK
    r = window // (2 * d)
    assert r == (kb - qb) // 2 and sub_len >= 2 * qb
    tq = min(ATTN_Q_TILE, sub_len)
    n_tiles = sub_len // tq
    n_res = min(d, ATTN_Q_TILE // tq)
    halo_per_tile = tq // r
    n_halo_blocks = sub_len // r

    def main_spec(j):
        return pl.BlockSpec((None, None, n_res, tq, ATTN_WIDTH),
                            lambda bi, ri, ti: (j, bi, ri, ti, 0))

    def prev_spec(j):
        return pl.BlockSpec(
            (None, None, n_res, r, ATTN_WIDTH),
            lambda bi, ri, ti: (j, bi, ri, jnp.maximum(ti * halo_per_tile - 1, 0), 0))

    def next_spec(j):
        return pl.BlockSpec(
            (None, None, n_res, r, ATTN_WIDTH),
            lambda bi, ri, ti: (j, bi, ri, jnp.minimum((ti + 1) * halo_per_tile, n_halo_blocks - 1),
                                0))

    def out_spec(width):
        return pl.BlockSpec((None, n_res, tq, width), lambda bi, ri, ti: (bi, ri, ti, 0))

    has_halo = n_tiles > 1

    def kv_specs(j):
        return [main_spec(j), prev_spec(j), next_spec(j)] if has_halo else [main_spec(j)]

    return pl.pallas_call(
        functools.partial(_attn_body, sub_len // qb, has_halo),
        grid=(batch, d // n_res, n_tiles),
        in_specs=[main_spec(0)] + kv_specs(1) + kv_specs(2) + [_const_spec((3, N_HEADS, qb, kb))],
        out_specs=[out_spec(ATTN_WIDTH), out_spec(STAT_WIDTH)],
        out_shape=[jax.ShapeDtypeStruct((batch, d, sub_len, ATTN_WIDTH), jnp.bfloat16),
                   jax.ShapeDtypeStruct((batch, d, sub_len, STAT_WIDTH), jnp.float32)],
        scratch_shapes=[pltpu.VMEM((n_res, tq + 2 * r, ATTN_WIDTH), jnp.bfloat16),
                        pltpu.VMEM((n_res, tq + 2 * r, 2 * ATTN_WIDTH), jnp.bfloat16),
                        pltpu.VMEM((ATTN_SCORE_BUFFERS, 2 * qb, kb), jnp.float32),
                        pltpu.VMEM((ATTN_SCORE_BUFFERS, 2 * qb, kb), jnp.bfloat16)],
        compiler_params=_params(3),
        name=f"attn_group{group}",
    )(*([qkv_g] * (7 if has_halo else 3)), _attn_bias(group))


def _head_expand_matrix():
    k = np.arange(2 * STAT_WIDTH)[:, None] % STAT_WIDTH
    c = np.arange(ATTN_WIDTH)[None, :] // HEAD_DIM
    return jnp.asarray(k == c, dtype=jnp.bfloat16)


def _attn_out_ln_body(x_ref, o0_ref, o1_ref, o2_ref, l0_ref, l1_ref, l2_ref, e_ref, w_ref,
                      gain_ref, bias_ref, out_ref, o_scr, l_scr, pad_scr):
    o_refs = (o0_ref, o1_ref, o2_ref)
    l_refs = (l0_ref, l1_ref, l2_ref)

    def token_order(ref, scr, g, r0, ts):
        assert DIL_CONFIGS[0][1] == 1 and all(dil > 1 for _, dil in DIL_CONFIGS[1:])
        d, _, width = ref.shape
        if d == 1:
            return ref[0, r0:r0 + ts, :].astype(jnp.float32)
        n_slabs = width // V7X_LANES
        n = ts // d
        if d % V7X_SUBLANES == 0 and scr is o_scr:
            pitch = n + V7X_SUBLANES
            base = (r0 // ts) * d * pitch
            for r in range(d):
                piece = ref[r, r0 // d:(r0 + ts) // d, :].astype(jnp.float32)
                for c in range(n_slabs):
                    pad_scr[c, base + r * pitch:base + r * pitch + n, :] = (
                        piece[:, c * V7X_LANES:(c + 1) * V7X_LANES])
            return jnp.concatenate(
                [jnp.concatenate([pad_scr[c, pl.ds(base + i, d, stride=pitch), :] for i in range(n)],
                                 axis=0)
                 for c in range(n_slabs)], axis=1)
        for r in range(d):
            piece = ref[r, r0 // d:(r0 + ts) // d, :].astype(jnp.float32)
            for c in range(n_slabs):
                scr[g - 1, c, pl.ds(r0 + r, n, stride=d), :] = (
                    piece[:, c * V7X_LANES:(c + 1) * V7X_LANES])
        return jnp.concatenate([scr[g - 1, c, r0:r0 + ts, :] for c in range(n_slabs)], axis=1)

    ts = OUT_SUB_TILE
    is_head = lax.broadcasted_iota(jnp.int32, (ts, STAT_WIDTH), 1) < N_HEADS
    for r0 in range(0, x_ref.shape[0], ts):
        ms = [token_order(l_refs[g], l_scr, g, r0, ts) for g in range(N_DIL_GROUPS)]
        ls = [pltpu.roll(m, STAT_WIDTH - N_HEADS, 1) for m in ms]
        top = jnp.maximum(jnp.maximum(ms[0], ms[1]), ms[2])
        es = [jnp.exp2(m - top) for m in ms]
        inv = 1.0 / (es[0] * ls[0] + es[1] * ls[1] + es[2] * ls[2])
        acc = None
        for g in range(N_DIL_GROUPS):
            wt = jnp.where(is_head, es[g] * inv, 0.0)
            hi = wt.astype(jnp.bfloat16)
            lo = (wt - hi.astype(jnp.float32)).astype(jnp.bfloat16)
            w_full = _dot(jnp.concatenate([hi, lo], axis=1), e_ref[...])
            term = w_full * token_order(o_refs[g], o_scr, g, r0, ts)
            acc = term if acc is None else acc + term
        m = _dot(acc.astype(jnp.bfloat16), w_ref[...])
        out_ref[r0:r0 + ts, :] = _layer_norm(ALPHA * x_ref[r0:r0 + ts, :] + m,
                                             gain_ref[...], bias_ref[...])


def _attn_out_ln(x, os_, stats, w_out, gain, bias, seq_len):
    m = x.shape[0]
    tm = OUT_ROW_TILE
    tiles_per_seq = seq_len // tm
    row_spec = pl.BlockSpec((tm, D_MODEL), lambda i: (i, 0))
    d_max = max(d for _, d in DIL_CONFIGS)
    pad_rows = tm + (tm // OUT_SUB_TILE) * d_max * V7X_SUBLANES

    def group_spec(d, width):
        return pl.BlockSpec((None, d, tm // d, width),
                            lambda i: (i // tiles_per_seq, 0, i % tiles_per_seq, 0))

    return pl.pallas_call(
        _attn_out_ln_body,
        grid=(m // tm,),
        in_specs=[row_spec]
                 + [group_spec(d, ATTN_WIDTH) for _, d in DIL_CONFIGS]
                 + [group_spec(d, STAT_WIDTH) for _, d in DIL_CONFIGS]
                 + [_const_spec((2 * STAT_WIDTH, ATTN_WIDTH)),
                    _const_spec((ATTN_WIDTH, D_MODEL)),
                    _const_spec((1, D_MODEL)), _const_spec((1, D_MODEL))],
        out_specs=row_spec,
        out_shape=jax.ShapeDtypeStruct((m, D_MODEL), jnp.float32),
        scratch_shapes=[
            pltpu.VMEM((N_DIL_GROUPS - 1, ATTN_WIDTH // V7X_LANES, tm, V7X_LANES), jnp.float32),
            pltpu.VMEM((N_DIL_GROUPS - 1, STAT_WIDTH // V7X_LANES, tm, V7X_LANES), jnp.float32),
            pltpu.VMEM((ATTN_WIDTH // V7X_LANES, pad_rows, V7X_LANES), jnp.float32)],
        compiler_params=_params(1),
        name="attn_out_ln",
    )(x, *os_, *stats, _head_expand_matrix(), w_out, gain, bias)


def kernel(x, ffn1_w_gate, ffn1_w_up, ffn1_w_down, ffn2_w_gate, ffn2_w_up, ffn2_w_down,
           ln_gain, ln_bias, pool_w_in, pool_w_group, pool_scale, pool_w_out,
           attn_w_qkv, attn_w_out):
    batch, seq_len, _ = x.shape
    n_mixers = 2
    h = x.reshape(batch * seq_len, D_MODEL)

    def ln_params(i, j):
        return ln_gain[i, j].reshape(1, D_MODEL), ln_bias[i, j].reshape(1, D_MODEL)

    ffn_stacks = [(ffn1_w_gate, ffn1_w_up, ffn1_w_down), (ffn2_w_gate, ffn2_w_up, ffn2_w_down)]
    ffn_order = [(which, i) for i in range(DEPTH) for which in range(2)]
    ffn_ready = [None]

    def ffn(k, h, gain, bias, extra=()):
        which, layer = ffn_order[k]
        convert = []
        if k + 1 < len(ffn_order):
            nxt_which, nxt_layer = ffn_order[k + 1]
            convert = [(w, nxt_layer) for w in ffn_stacks[nxt_which]]
        n_next = len(convert)
        convert = convert + list(extra)
        if ffn_ready[0] is None:
            h, conv = _ffn_ln(h, *ffn_stacks[which], gain, bias, layer=layer, convert=convert)
        else:
            h, conv = _ffn_ln(h, *ffn_ready[0], gain, bias, convert=convert)
        ffn_ready[0] = conv[:n_next]
        return h, conv[n_next:]

    w_qkv_ready = {}
    for i in range(DEPTH):
        if i % n_mixers == 0:
            p = i // n_mixers
            group_stack = pool_w_group.reshape(-1, D_MODEL, POOL_GROUP_DIM)
            h, (w_in, w_group, w_out) = ffn(2 * i, h, *ln_params(i, 0),
                                            [(pool_w_in, p), (group_stack, p), (pool_w_out, p)])
            assert i + 1 < DEPTH
            h, (w_qkv_ready[(i + 1) // n_mixers],) = _pool_ln(
                h.reshape(batch, seq_len, D_MODEL), w_in,
                w_group.reshape(N_POOL_GROUPS, POOL_GROUP_DIM, POOL_GROUP_DIM),
                pool_scale[p].reshape(1, D_MODEL), w_out, *ln_params(i, 1),
                (attn_w_qkv, (i + 1) // n_mixers))
            h = h.reshape(batch * seq_len, D_MODEL)
        else:
            a = i // n_mixers
            h, (w_out,) = ffn(2 * i, h, *ln_params(i, 0), [(attn_w_out, a)])
            qkv = _qkv_proj(h, w_qkv_ready[a], batch, seq_len)
            outs = [_attn_group(qkv[g], g) for g in range(N_DIL_GROUPS)]
            h = _attn_out_ln(h, [o for o, _ in outs], [l for _, l in outs],
                             w_out, *ln_params(i, 1), seq_len)
        h, _ = ffn(2 * i + 1, h, *ln_params(i, 2))
    return h.reshape(batch, seq_len, D_MODEL)
```
